```python
import math
import jax, jax.numpy as jnp
from jax import lax
import numpy as np

D_MODEL = 2048
BATCH = 2
SEQ = 4096
DEPTH = 4
DEC_BATCH = 8
DEC_SEQ = 8
PAST_LEN = 16384
PAGE_SIZE = 128

MIX_WIDTH = D_MODEL
GLA_HEADS = 4
GLA_DK = D_MODEL // 32
GLA_DV = D_MODEL // 16
GLA_WIDTH = GLA_HEADS * GLA_DV
GLA_QK = GLA_HEADS * GLA_DK
GLA_GATE_RANK = 16
GLA_TAU = 16.0
GLA_CHUNK = 64
MOBA_HEADS = 8
HEAD_DIM = D_MODEL // 16
MOBA_WIDTH = MOBA_HEADS * HEAD_DIM
MOBA_BLOCK = 256
MOBA_TOPK = 3
MOBA_Q_CHUNK = 32
ROPE_THETA = 10000.0
CONV_CH = MIX_WIDTH - GLA_WIDTH - MOBA_WIDTH
CONV_WIDTH = 31
D_FF = ((8 * D_MODEL + 767) // 768) * 256
RMS_EPS = 1e-6
LN_EPS = 1e-5

_IN_SIZES = (GLA_QK, GLA_QK, GLA_WIDTH, GLA_WIDTH, GLA_GATE_RANK, MOBA_WIDTH, MOBA_WIDTH, MOBA_WIDTH, CONV_CH, CONV_CH)
IN_COLS = sum(_IN_SIZES)
IN_SPLITS = tuple(int(s) for s in np.cumsum(_IN_SIZES)[:-1])

kernel_name = 'hymba_gla_moba_conformer_decoder_step'


def rmsnorm(x, g):
    x32 = x.astype(jnp.float32)
    y = x32 * lax.rsqrt(jnp.mean(x32 * x32, axis=-1, keepdims=True) + RMS_EPS)
    return (y * g.astype(jnp.float32)).astype(x.dtype)


def layernorm(x, g, b):
    x32 = x.astype(jnp.float32)
    mu = jnp.mean(x32, axis=-1, keepdims=True)
    xc = x32 - mu
    var = jnp.mean(xc * xc, axis=-1, keepdims=True)
    y = xc * lax.rsqrt(var + LN_EPS) * g.astype(jnp.float32) + b.astype(jnp.float32)
    return y.astype(x.dtype)


def rotary(x, pos0):
    T, dh = x.shape[1], x.shape[-1]
    pos = (pos0 + jnp.arange(T)).astype(jnp.float32)
    inv = ROPE_THETA ** (-jnp.arange(0, dh, 2, dtype=jnp.float32) / dh)
    ang = pos[:, None] * inv[None, :]
    cos = jnp.cos(ang)[None, :, None, :]
    sin = jnp.sin(ang)[None, :, None, :]
    x1, x2 = jnp.split(x.astype(jnp.float32), 2, axis=-1)
    return jnp.concatenate([x1 * cos - x2 * sin, x2 * cos + x1 * sin], axis=-1).astype(x.dtype)


def gla_scan(q, k, v, log_a, s0):
    B, T, H, dk = q.shape
    dv = v.shape[-1]
    C = math.gcd(T, GLA_CHUNK)
    n = T // C

    def chunks(a):
        return a.astype(jnp.float32).reshape(B, n, C, H, a.shape[-1]).transpose(1, 0, 3, 2, 4)

    causal = jnp.tril(jnp.ones((C, C), dtype=bool))

    def step(S, inp):
        qc, kc, vc, lac = inp
        b = jnp.cumsum(lac, axis=2)
        diff = b[:, :, :, None, :] - b[:, :, None, :, :]
        decay = jnp.exp(jnp.where(causal[None, None, :, :, None], diff, -jnp.inf))
        A = jnp.einsum('bhid,bhjd,bhijd->bhij', qc, kc, decay)
        o = jnp.einsum('bhij,bhjv->bhiv', A, vc) + jnp.einsum('bhid,bhdv->bhiv', qc * jnp.exp(b), S)
        b_last = b[:, :, -1, :]
        S = jnp.exp(b_last)[..., None] * S + jnp.einsum('bhjd,bhjv->bhdv', kc * jnp.exp(b_last[:, :, None, :] - b), vc)
        return S, o

    S, o = lax.scan(step, s0.astype(jnp.float32), (chunks(q), chunks(k), chunks(v), chunks(log_a)))
    o = o.transpose(1, 0, 3, 2, 4).reshape(B, T, H, dv)
    return o, S


def moba_attention(q, k, v, q_pos0):
    B, H, Q, dh = q.shape
    T = k.shape[2]
    NB = -(-T // MOBA_BLOCK)
    pad = NB * MOBA_BLOCK - T
    kb = jnp.pad(k, ((0, 0), (0, 0), (0, pad), (0, 0))).reshape(B, H, NB, MOBA_BLOCK, dh)
    vb = jnp.pad(v, ((0, 0), (0, 0), (0, pad), (0, 0))).reshape(B, H, NB, MOBA_BLOCK, dh)
    kmean = jnp.mean(kb.astype(jnp.float32), axis=3)
    ksel = min(MOBA_TOPK, NB - 1)
    scale = dh ** -0.5
    bi = jnp.arange(B)[:, None, None, None]
    hi = jnp.arange(H)[None, :, None, None]
    offs = jnp.arange(MOBA_BLOCK)
    C = math.gcd(Q, MOBA_Q_CHUNK)
    nC = Q // C

    def attend(args):
        qc, pos = args
        own = pos // MOBA_BLOCK
        own_b = jnp.broadcast_to(own[None, None, :, None], (B, H, C, 1))
        q32 = qc.astype(jnp.float32)
        if ksel > 0:
            s = jnp.einsum('bhcd,bhnd->bhcn', q32, kmean)
            past = jnp.arange(NB)[None, None, None, :] < own[None, None, :, None]
            s = jnp.where(past, s, -jnp.inf)
            _, idx = lax.top_k(s, ksel)
            blk = jnp.concatenate([idx, own_b], axis=-1)
            valid = jnp.concatenate([idx < own[None, None, :, None], jnp.ones((B, H, C, 1), dtype=bool)], axis=-1)
        else:
            blk = own_b
            valid = jnp.ones((B, H, C, 1), dtype=bool)
        kg = kb[bi, hi, blk].astype(jnp.float32)
        vg = vb[bi, hi, blk].astype(jnp.float32)
        kpos = blk[..., None] * MOBA_BLOCK + offs
        allowed = valid[..., None] & (kpos <= pos[None, None, :, None, None])
        logits = jnp.einsum('bhcd,bhcnmd->bhcnm', q32, kg) * scale
        logits = jnp.where(allowed, logits, -jnp.inf)
        nsel = blk.shape[-1]
        p = jax.nn.softmax(logits.reshape(B, H, C, nsel * MOBA_BLOCK), axis=-1).reshape(B, H, C, nsel, MOBA_BLOCK)
        return jnp.einsum('bhcnm,bhcnmd->bhcd', p, vg).astype(qc.dtype)

    qs = q.reshape(B, H, nC, C, dh).transpose(2, 0, 1, 3, 4)
    ps = (q_pos0 + jnp.arange(Q)).reshape(nC, C)
    out = lax.map(attend, (qs, ps))
    return out.transpose(1, 2, 0, 3, 4).reshape(B, H, Q, dh)


def block(x, c, pos0, k_past, v_past, gla_s0, conv_buf,
          w_ada, b_ada, g_pre_mix, g_post_mix, g_pre_ffn, g_post_ffn,
          w_in, w_a2, b_a, gla_norm_g, conv_w, conv_b, conv_ln_g, conv_ln_b,
          w_out, w_ffn1, w_ffn3, w_ffn2):
    B, T, _ = x.shape
    mod = jnp.dot(jax.nn.silu(c), w_ada) + b_ada
    sh_m, sc_m, gt_m, sh_f, sc_f, gt_f = [m[:, None, :] for m in jnp.split(mod, 6, axis=-1)]

    h = rmsnorm(x, g_pre_mix) * (1 + sc_m) + sh_m
    proj = h @ w_in
    gq, gk, gv, gg, ga, mq, mk, mv, cu, cg = jnp.split(proj, IN_SPLITS, axis=-1)

    log_a = jax.nn.log_sigmoid((ga @ w_a2 + b_a).astype(jnp.float32)) / GLA_TAU
    o_gla, gla_state = gla_scan(gq.reshape(B, T, GLA_HEADS, GLA_DK) * GLA_DK ** -0.5,
                                gk.reshape(B, T, GLA_HEADS, GLA_DK),
                                gv.reshape(B, T, GLA_HEADS, GLA_DV),
                                log_a.reshape(B, T, GLA_HEADS, GLA_DK), gla_s0)
    o_gla = (rmsnorm(o_gla, gla_norm_g).reshape(B, T, GLA_WIDTH) * jax.nn.silu(gg)).astype(x.dtype)

    mq = rotary(mq.reshape(B, T, MOBA_HEADS, HEAD_DIM), pos0)
    mk = rotary(mk.reshape(B, T, MOBA_HEADS, HEAD_DIM), pos0)
    mv = mv.reshape(B, T, MOBA_HEADS, HEAD_DIM)
    if k_past is None:
        k_all, v_all = mk, mv
    else:
        k_all = jnp.concatenate([k_past.astype(mk.dtype), mk], axis=1)
        v_all = jnp.concatenate([v_past.astype(mv.dtype), mv], axis=1)
    o_moba = moba_attention(mq.transpose(0, 2, 1, 3), k_all.transpose(0, 2, 1, 3),
                            v_all.transpose(0, 2, 1, 3), pos0)
    o_moba = o_moba.transpose(0, 2, 1, 3).reshape(B, T, MOBA_WIDTH)

    glu = cu * jax.nn.sigmoid(cg)
    full = jnp.concatenate([conv_buf.astype(glu.dtype), glu], axis=1)
    y = lax.conv_general_dilated(full, conv_w[:, None, :].astype(full.dtype), (1,), 'VALID',
                                 dimension_numbers=('NWC', 'WIO', 'NWC'),
                                 feature_group_count=CONV_CH) + conv_b
    new_buf = full[:, full.shape[1] - (CONV_WIDTH - 1):]
    o_conv = jax.nn.silu(layernorm(y, conv_ln_g, conv_ln_b))

    mixed = jnp.concatenate([o_gla, o_moba, o_conv.astype(x.dtype)], axis=-1) @ w_out
    x = x + gt_m * rmsnorm(mixed, g_post_mix)

    h = rmsnorm(x, g_pre_ffn) * (1 + sc_f) + sh_f
    f = (jax.nn.silu(h @ w_ffn1) * (h @ w_ffn3)) @ w_ffn2
    x = x + gt_f * rmsnorm(f, g_post_ffn)
    return x, mk, mv, gla_state.astype(x.dtype), new_buf


def setup_inputs(seed: int = 0) -> dict:
    key = jax.random.key(seed)
    ks = iter(jax.random.split(key, 40))
    f32 = jnp.float32

    def nrm(shape, scale):
        return scale * jax.random.normal(next(ks), shape, f32)

    n_pages = PAST_LEN // PAGE_SIZE
    n_used = DEC_BATCH * n_pages
    n_phys = n_used + max(1, n_used // 4)
    page_table = jax.random.permutation(next(ks), n_phys)[:n_used].reshape(DEC_BATCH, n_pages).astype(jnp.int32)
    return {
        'x_prompt': nrm((BATCH, SEQ, D_MODEL), 1.0),
        'x_sample': nrm((DEC_BATCH, DEC_SEQ, D_MODEL), 1.0),
        'c_prompt': nrm((BATCH, D_MODEL), 1.0),
        'c_sample': nrm((DEC_BATCH, D_MODEL), 1.0),
        'cache_k': nrm((DEPTH, n_phys, PAGE_SIZE, MOBA_HEADS, HEAD_DIM), 1.0),
        'cache_v': nrm((DEPTH, n_phys, PAGE_SIZE, MOBA_HEADS, HEAD_DIM), 1.0),
        'state_gla': nrm((DEPTH, DEC_BATCH, GLA_HEADS, GLA_DK, GLA_DV), 1.0),
        'state_conv': nrm((DEPTH, DEC_BATCH, CONV_WIDTH - 1, CONV_CH), 0.5),
        'page_table': page_table,
        'w_ada': nrm((DEPTH, D_MODEL, 6 * D_MODEL), D_MODEL ** -0.5),
        'b_ada': nrm((DEPTH, 6 * D_MODEL), 0.01),
        'g_pre_mix': 1.0 + nrm((DEPTH, D_MODEL), 0.05),
        'g_post_mix': 1.0 + nrm((DEPTH, D_MODEL), 0.05),
        'g_pre_ffn': 1.0 + nrm((DEPTH, D_MODEL), 0.05),
        'g_post_ffn': 1.0 + nrm((DEPTH, D_MODEL), 0.05),
        'w_in': nrm((DEPTH, D_MODEL, IN_COLS), D_MODEL ** -0.5),
        'w_a2': nrm((DEPTH, GLA_GATE_RANK, GLA_QK), GLA_GATE_RANK ** -0.5),
        'b_a': nrm((DEPTH, GLA_QK), 0.1),
        'gla_norm_g': 1.0 + nrm((DEPTH, GLA_DV), 0.05),
        'conv_w': nrm((DEPTH, CONV_WIDTH, CONV_CH), CONV_WIDTH ** -0.5),
        'conv_b': nrm((DEPTH, CONV_CH), 0.01),
        'conv_ln_g': 1.0 + nrm((DEPTH, CONV_CH), 0.05),
        'conv_ln_b': nrm((DEPTH, CONV_CH), 0.01),
        'w_out': nrm((DEPTH, MIX_WIDTH, D_MODEL), MIX_WIDTH ** -0.5),
        'w_ffn1': nrm((DEPTH, D_MODEL, D_FF), D_MODEL ** -0.5),
        'w_ffn3': nrm((DEPTH, D_MODEL, D_FF), D_MODEL ** -0.5),
        'w_ffn2': nrm((DEPTH, D_FF, D_MODEL), D_FF ** -0.5),
    }


def reference(x_prompt, x_sample, c_prompt, c_sample, cache_k, cache_v, state_gla, state_conv, page_table,
              w_ada, b_ada, g_pre_mix, g_post_mix, g_pre_ffn, g_post_ffn,
              w_in, w_a2, b_a, gla_norm_g, conv_w, conv_b, conv_ln_g, conv_ln_b,
              w_out, w_ffn1, w_ffn3, w_ffn2):
    Bp = x_prompt.shape[0]
    Bs = x_sample.shape[0]
    past = page_table.shape[1] * cache_k.shape[2]
    yp, ys = x_prompt, x_sample
    kp_l, vp_l, sp_l, cp_l = [], [], [], []
    ks_l, vs_l, ss_l, cs_l = [], [], [], []
    for l in range(DEPTH):
        lp = (w_ada[l], b_ada[l], g_pre_mix[l], g_post_mix[l], g_pre_ffn[l], g_post_ffn[l],
              w_in[l], w_a2[l], b_a[l], gla_norm_g[l], conv_w[l], conv_b[l], conv_ln_g[l], conv_ln_b[l],
              w_out[l], w_ffn1[l], w_ffn3[l], w_ffn2[l])
        gla0 = jnp.zeros((Bp, GLA_HEADS, GLA_DK, GLA_DV), x_prompt.dtype)
        buf0 = jnp.zeros((Bp, CONV_WIDTH - 1, CONV_CH), x_prompt.dtype)
        yp, kp, vp, sp, cp = block(yp, c_prompt, 0, None, None, gla0, buf0, *lp)
        k_past = cache_k[l][page_table].reshape(Bs, past, MOBA_HEADS, HEAD_DIM)
        v_past = cache_v[l][page_table].reshape(Bs, past, MOBA_HEADS, HEAD_DIM)
        ys, ksn, vsn, ssn, csn = block(ys, c_sample, past, k_past, v_past, state_gla[l], state_conv[l], *lp)
        kp_l.append(kp); vp_l.append(vp); sp_l.append(sp); cp_l.append(cp)
        ks_l.append(ksn); vs_l.append(vsn); ss_l.append(ssn); cs_l.append(csn)
    return (yp, ys,
            jnp.stack(kp_l), jnp.stack(vp_l), jnp.stack(sp_l), jnp.stack(cp_l),
            jnp.stack(ks_l), jnp.stack(vs_l), jnp.stack(ss_l), jnp.stack(cs_l))
```

```python
import functools
import math

import jax
import jax.numpy as jnp
from jax import lax
from jax.experimental import pallas as pl
from jax.experimental.pallas import tpu as pltpu

F32 = jnp.float32
BF16 = jnp.bfloat16
HIGHEST = lax.Precision.HIGHEST

D_MODEL = 2048
DEPTH = 4
GLA_HEADS = 4
GLA_DK = 64
GLA_DV = 128
GLA_QK = GLA_HEADS * GLA_DK
GLA_WIDTH = GLA_HEADS * GLA_DV
GLA_GATE_RANK = 16
GLA_TAU = 16.0
MOBA_HEADS = 8
HEAD_DIM = 128
MOBA_WIDTH = MOBA_HEADS * HEAD_DIM
MOBA_BLOCK = 256
MOBA_TOPK = 3
ROPE_THETA = 10000.0
CONV_CH = 512
CONV_WIDTH = 31
D_FF = 5632
RMS_EPS = 1e-6
LN_EPS = 1e-5
PAGE_SIZE = 128
PAGES_PER_BLOCK = MOBA_BLOCK // PAGE_SIZE

LANES = 128
SUBLANES = 8
VMEM_LIMIT_CAP = 60 * 1024 * 1024

PROJ_COLS = 2 * GLA_QK + 2 * GLA_WIDTH + 3 * MOBA_WIDTH + 2 * CONV_CH
OFF_GQ, OFF_GK, OFF_GV, OFF_GG = 0, 256, 512, 1024
OFF_MQ, OFF_MK, OFF_MV = 1536, 2560, 3584
OFF_CU, OFF_CG = 4608, 5120
GATE_COL0 = 2 * GLA_QK + 2 * GLA_WIDTH
PROJ_TN = 512
MASK_VALUE = -1e30

GLA_CHUNK = 16


def _vmem_limit(*nbytes):
    need = int(sum(nbytes) * 1.25) + (4 << 20)
    return min(max(need, 16 << 20), VMEM_LIMIT_CAP)


def _nbytes(shape, dtype):
    return math.prod(shape) * jnp.dtype(dtype).itemsize


def _sigmoid(x):
    return 1.0 / (1.0 + jnp.exp(-x))


def _silu(x):
    return x * _sigmoid(x)


def _rms(x, g):
    return x * lax.rsqrt(jnp.mean(x * x, axis=-1, keepdims=True) + RMS_EPS) * g


def _dot_nt(a, b, **kw):
    return lax.dot_general(a, b, (((1,), (1,)), ((), ())), preferred_element_type=F32, **kw)


def _dot_tn(a, b, **kw):
    return lax.dot_general(a, b, (((0,), (0,)), ((), ())), preferred_element_type=F32, **kw)


MOD_TN = 1024


def _mod_kernel(c_ref, w_ref, b_ref, o_ref):
    a = _silu(c_ref[...]).astype(BF16)
    o_ref[...] = jnp.dot(a, w_ref[...].astype(BF16), preferred_element_type=F32) + b_ref[...]


def _modulation(c_all, w_ada, b_ada):
    rows = c_all.shape[0]
    n = w_ada.shape[-1]
    return pl.pallas_call(
        _mod_kernel,
        grid=(DEPTH, n // MOD_TN),
        in_specs=[
            pl.BlockSpec((rows, D_MODEL), lambda l, j: (0, 0)),
            pl.BlockSpec((None, D_MODEL, MOD_TN), lambda l, j: (l, 0, j)),
            pl.BlockSpec((None, 1, MOD_TN), lambda l, j: (l, 0, j)),
        ],
        out_specs=pl.BlockSpec((None, rows, MOD_TN), lambda l, j: (l, 0, j)),
        out_shape=jax.ShapeDtypeStruct((DEPTH, rows, n), F32),
        compiler_params=pltpu.CompilerParams(
            dimension_semantics=("arbitrary", "arbitrary"),
            vmem_limit_bytes=_vmem_limit(2 * _nbytes((D_MODEL, MOD_TN), F32), _nbytes((D_MODEL, MOD_TN), BF16))),
        name="adaln_mod",
    )(c_all, w_ada, b_ada.reshape(DEPTH, 1, n))


def _inproj_kernel(x_ref, sh_ref, sc_ref, g_ref, w_ref, wga_ref, cos_ref, sin_ref, o_ref, ga_ref, h_scr):
    j = pl.program_id(1)

    @pl.when(j == 0)
    def _():
        h = _rms(x_ref[...], g_ref[...]) * (1.0 + sc_ref[...]) + sh_ref[...]
        hb = h.astype(BF16)
        h_scr[...] = hb
        ga_ref[...] = jnp.dot(hb, wga_ref[...], preferred_element_type=F32)

    acc = jnp.dot(h_scr[...], w_ref[...], preferred_element_type=F32)
    is_rot = (j >= OFF_MQ // PROJ_TN) & (j < OFF_MV // PROJ_TN)

    @pl.when(is_rot)
    def _():
        cos2, sin2 = cos_ref[...], sin_ref[...]
        for s in range(PROJ_TN // HEAD_DIM):
            a = acc[:, s * HEAD_DIM:(s + 1) * HEAD_DIM]
            o_ref[:, s * HEAD_DIM:(s + 1) * HEAD_DIM] = a * cos2 + pltpu.roll(a, HEAD_DIM // 2, 1) * sin2

    @pl.when(jnp.logical_not(is_rot))
    def _():
        o_ref[...] = acc


def _inproj(x, sh, sc, g, w_main, w_ga, cos2, sin2, *, tm, mod_spec, rope_spec):
    n = x.shape[0]
    vm = _vmem_limit(2 * _nbytes((tm, D_MODEL), F32), _nbytes((tm, D_MODEL), BF16), 2 * _nbytes((D_MODEL, PROJ_TN), BF16),
                     2 * _nbytes((tm, PROJ_TN), F32), 2 * _nbytes((D_MODEL, LANES), BF16), 6 * _nbytes((tm, LANES), F32),
                     4 * _nbytes(mod_spec.block_shape[-2:], F32) if mod_spec.block_shape[-2] != 1 else 0)
    return pl.pallas_call(
        _inproj_kernel,
        grid=(n // tm, PROJ_COLS // PROJ_TN),
        in_specs=[
            pl.BlockSpec((tm, D_MODEL), lambda i, j: (i, 0)),
            mod_spec, mod_spec,
            pl.BlockSpec((1, D_MODEL), lambda i, j: (0, 0)),
            pl.BlockSpec((D_MODEL, PROJ_TN), lambda i, j: (0, j)),
            pl.BlockSpec((D_MODEL, LANES), lambda i, j: (0, 0)),
            rope_spec, rope_spec,
        ],
        out_specs=[
            pl.BlockSpec((tm, PROJ_TN), lambda i, j: (i, j)),
            pl.BlockSpec((tm, LANES), lambda i, j: (i, 0)),
        ],
        out_shape=[jax.ShapeDtypeStruct((n, PROJ_COLS), F32), jax.ShapeDtypeStruct((n, LANES), F32)],
        scratch_shapes=[pltpu.VMEM((tm, D_MODEL), BF16)],
        compiler_params=pltpu.CompilerParams(dimension_semantics=("arbitrary", "arbitrary"), vmem_limit_bytes=vm),
        name="inproj",
    )(x, sh, sc, g, w_main, w_ga, cos2, sin2)


def _gla_kernel(q_ref, k_ref, v_ref, gg_ref, ga_ref, wa2_ref, ba_ref, gn_ref, s0_ref, o_ref, st_ref, st_scr, la_scr, *, tb, chunk):
    t = pl.program_id(1)

    @pl.when(t == 0)
    def _():
        for h in range(GLA_HEADS):
            st_scr[h] = s0_ref[h].T

    xg = jnp.dot(ga_ref[...], wa2_ref[...], preferred_element_type=F32, precision=HIGHEST) + ba_ref[...]
    la_scr[...] = -(jnp.maximum(-xg, 0.0) + jnp.log1p(jnp.exp(-jnp.abs(xg)))) / GLA_TAU

    row = lax.broadcasted_iota(jnp.int32, (chunk, chunk), 0)
    col = lax.broadcasted_iota(jnp.int32, (chunk, chunk), 1)
    tril = (row >= col).astype(F32)
    gn = gn_ref[...]

    def chunk_body(c, carry):
        r0 = pl.multiple_of(c * chunk, chunk)
        rows = pl.ds(r0, chunk)
        b_all = jnp.dot(tril, la_scr[rows, :], preferred_element_type=F32, precision=HIGHEST)
        q_all = q_ref[rows, :] * (GLA_DK ** -0.5)
        k_all = k_ref[rows, :]
        for h in range(GLA_HEADS):
            dsl = slice(h * GLA_DK, (h + 1) * GLA_DK)
            vsl = slice(h * GLA_DV, (h + 1) * GLA_DV)
            bh, qh, kh = b_all[:, dsl], q_all[:, dsl], k_all[:, dsl]
            vh = v_ref[rows, vsl]
            st = st_scr[h]
            o = _dot_nt(qh * jnp.exp(bh), st)
            gmat = jnp.zeros((chunk, chunk), F32)
            for jj in range(chunk):
                e = jnp.exp(jnp.minimum(bh - bh[jj:jj + 1, :], 0.0))
                cj = jnp.sum(qh * kh[jj:jj + 1, :] * e, axis=1, keepdims=True)
                gmat = jnp.where(col == jj, cj, gmat)
            gmat = jnp.where(row >= col, gmat, 0.0)
            o = o + jnp.dot(gmat, vh, preferred_element_type=F32)
            bl = bh[chunk - 1:chunk, :]
            st_scr[h] = jnp.exp(bl) * st + _dot_tn(vh, kh * jnp.exp(bl - bh))
            gate = gg_ref[rows, vsl]
            o_ref[rows, vsl] = (_rms(o, gn) * _silu(gate)).astype(o_ref.dtype)
        return carry

    lax.fori_loop(0, tb // chunk, chunk_body, 0)

    @pl.when(t == pl.num_programs(1) - 1)
    def _():
        for h in range(GLA_HEADS):
            st_ref[h] = st_scr[h].T


def _gla(proj, ga, wa2, ba, gn, s0, *, seqs, seq_len, tb, chunk, act_dtype):
    n = proj.shape[0]
    steps = seq_len // tb
    row = lambda b, t: b * steps + t
    kern = functools.partial(_gla_kernel, tb=tb, chunk=chunk)
    return pl.pallas_call(
        kern,
        grid=(seqs, steps),
        in_specs=[
            pl.BlockSpec((tb, GLA_QK), lambda b, t: (row(b, t), OFF_GQ // GLA_QK)),
            pl.BlockSpec((tb, GLA_QK), lambda b, t: (row(b, t), OFF_GK // GLA_QK)),
            pl.BlockSpec((tb, GLA_WIDTH), lambda b, t: (row(b, t), OFF_GV // GLA_WIDTH)),
            pl.BlockSpec((tb, GLA_WIDTH), lambda b, t: (row(b, t), OFF_GG // GLA_WIDTH)),
            pl.BlockSpec((tb, LANES), lambda b, t: (row(b, t), 0)),
            pl.BlockSpec((LANES, GLA_QK), lambda b, t: (0, 0)),
            pl.BlockSpec((1, GLA_QK), lambda b, t: (0, 0)),
            pl.BlockSpec((1, GLA_DV), lambda b, t: (0, 0)),
            pl.BlockSpec((None, GLA_HEADS, GLA_DK, GLA_DV), lambda b, t: (b, 0, 0, 0)),
        ],
        out_specs=[
            pl.BlockSpec((tb, GLA_WIDTH), lambda b, t: (row(b, t), 0)),
            pl.BlockSpec((None, GLA_HEADS, GLA_DK, GLA_DV), lambda b, t: (b, 0, 0, 0)),
        ],
        out_shape=[jax.ShapeDtypeStruct((n, GLA_WIDTH), act_dtype),
                   jax.ShapeDtypeStruct((seqs, GLA_HEADS, GLA_DK, GLA_DV), F32)],
        scratch_shapes=[pltpu.VMEM((GLA_HEADS, GLA_DV, GLA_DK), F32), pltpu.VMEM((tb, GLA_QK), F32)],
        compiler_params=pltpu.CompilerParams(dimension_semantics=("arbitrary", "arbitrary")),
        name="gla",
    )(proj, proj, proj, proj, ga, wa2, ba, gn, s0)


HALO = 32


def _conv_kernel(cu_ref, cg_ref, cw_ref, cb_ref, lg_ref, lb_ref, buf_ref, o_ref, nb_ref, ext, *, tb):
    t = pl.program_id(1)
    hist = CONV_WIDTH - 1

    @pl.when(t == 0)
    def _():
        ext[HALO - hist:HALO, :] = buf_ref[...]

    @pl.when(t > 0)
    def _():
        ext[0:HALO, :] = ext[tb:tb + HALO, :]

    ext[HALO:HALO + tb, :] = cu_ref[...] * _sigmoid(cg_ref[...])
    y = jnp.broadcast_to(cb_ref[...], (tb, CONV_CH))
    for w in range(CONV_WIDTH):
        y = y + ext[HALO - hist + w:HALO - hist + w + tb, :] * cw_ref[w:w + 1, :]
    nb_ref[...] = ext[HALO + tb - hist:HALO + tb, :]
    mu = jnp.mean(y, axis=-1, keepdims=True)
    yc = y - mu
    var = jnp.mean(yc * yc, axis=-1, keepdims=True)
    yn = yc * lax.rsqrt(var + LN_EPS) * lg_ref[...] + lb_ref[...]
    o_ref[...] = _silu(yn).astype(o_ref.dtype)


def _conv(proj, cw, cb, lg, lb, buf0, *, seqs, seq_len, tb, act_dtype):
    n = proj.shape[0]
    steps = seq_len // tb
    row = lambda b, t: b * steps + t
    hist = CONV_WIDTH - 1
    vec = pl.BlockSpec((1, CONV_CH), lambda b, t: (0, 0))
    return pl.pallas_call(
        functools.partial(_conv_kernel, tb=tb),
        grid=(seqs, steps),
        in_specs=[
            pl.BlockSpec((tb, CONV_CH), lambda b, t: (row(b, t), OFF_CU // CONV_CH)),
            pl.BlockSpec((tb, CONV_CH), lambda b, t: (row(b, t), OFF_CG // CONV_CH)),
            pl.BlockSpec((CONV_WIDTH, CONV_CH), lambda b, t: (0, 0)),
            vec, vec, vec,
            pl.BlockSpec((None, hist, CONV_CH), lambda b, t: (b, 0, 0)),
        ],
        out_specs=[
            pl.BlockSpec((tb, CONV_CH), lambda b, t: (row(b, t), 0)),
            pl.BlockSpec((None, hist, CONV_CH), lambda b, t: (b, 0, 0)),
        ],
        out_shape=[jax.ShapeDtypeStruct((n, CONV_CH), act_dtype), jax.ShapeDtypeStruct((seqs, hist, CONV_CH), F32)],
        scratch_shapes=[pltpu.VMEM((HALO + tb, CONV_CH), F32)],
        compiler_params=pltpu.CompilerParams(dimension_semantics=("arbitrary", "arbitrary")),
        name="conv",
    )(proj, proj, cw, cb, lg, lb, buf0)


def _top_blocks(s, n_past, lane):
    width = float(s.shape[-1])
    past = lane < n_past
    s = jnp.where(past, s, -jnp.inf)
    sel = jnp.zeros(s.shape, F32)
    for _ in range(MOBA_TOPK):
        m = jnp.max(s, axis=1, keepdims=True)
        first = jnp.min(jnp.where(s == m, lane, width), axis=1, keepdims=True)
        pick = (lane == first) & past
        sel = jnp.where(pick, 1.0, sel)
        s = jnp.where(pick, -jnp.inf, s)
    return sel


def _moba_prompt_kernel(q_ref, k_ref, v_ref, o_ref, kmean_scr, *, nblk):
    qi = pl.program_id(2)
    blk = MOBA_BLOCK

    @pl.when(qi == 0)
    def _():
        for n in range(nblk):
            kmean_scr[n:n + 1, :] = jnp.mean(k_ref[n * blk:(n + 1) * blk, :], axis=0, keepdims=True)

    q = q_ref[...]
    lane = lax.broadcasted_iota(jnp.int32, (blk, nblk), 1).astype(F32)
    sel = _top_blocks(_dot_nt(q, kmean_scr[...], precision=HIGHEST), qi.astype(F32), lane)
    qb = q.astype(BF16)
    row = lax.broadcasted_iota(jnp.int32, (blk, blk), 0)
    col = lax.broadcasted_iota(jnp.int32, (blk, blk), 1)
    causal = (row >= col).astype(F32)
    scale = HEAD_DIM ** -0.5

    def body(kj, carry):
        m, l, acc = carry
        rows = pl.ds(pl.multiple_of(kj * blk, blk), blk)
        logits = _dot_nt(qb, k_ref[rows, :].astype(BF16)) * scale
        picked = jnp.sum(jnp.where(lane == kj.astype(F32), sel, 0.0), axis=1, keepdims=True)
        allowed = jnp.where(kj == qi, causal, picked) > 0.5
        m_new = jnp.maximum(m, jnp.max(jnp.where(allowed, logits, MASK_VALUE), axis=1, keepdims=True))
        alpha = jnp.exp(m - m_new)
        p = jnp.where(allowed, jnp.exp(logits - m_new), 0.0)
        l = alpha * l + jnp.sum(p, axis=1, keepdims=True)
        acc = alpha * acc + jnp.dot(p.astype(BF16), v_ref[rows, :].astype(BF16), preferred_element_type=F32)
        return m_new, l, acc

    init = (jnp.full((blk, 1), MASK_VALUE, F32), jnp.zeros((blk, 1), F32), jnp.zeros((blk, HEAD_DIM), F32))
    _, l, acc = lax.fori_loop(0, qi + 1, body, init)
    o_ref[...] = (acc / l).astype(o_ref.dtype)


def _moba_prompt(proj, *, seqs, seq_len):
    n = proj.shape[0]
    nblk = seq_len // MOBA_BLOCK
    hb = lambda off: off // HEAD_DIM
    vm = _vmem_limit(4 * _nbytes((seq_len, HEAD_DIM), F32), 12 * _nbytes((MOBA_BLOCK, MOBA_BLOCK), F32))
    return pl.pallas_call(
        functools.partial(_moba_prompt_kernel, nblk=nblk),
        grid=(seqs, MOBA_HEADS, nblk),
        in_specs=[
            pl.BlockSpec((MOBA_BLOCK, HEAD_DIM), lambda b, h, i: (b * nblk + i, hb(OFF_MQ) + h)),
            pl.BlockSpec((seq_len, HEAD_DIM), lambda b, h, i: (b, hb(OFF_MK) + h)),
            pl.BlockSpec((seq_len, HEAD_DIM), lambda b, h, i: (b, hb(OFF_MV) + h)),
        ],
        out_specs=pl.BlockSpec((MOBA_BLOCK, HEAD_DIM), lambda b, h, i: (b * nblk + i, h)),
        out_shape=jax.ShapeDtypeStruct((n, MOBA_WIDTH), BF16),
        scratch_shapes=[pltpu.VMEM((nblk, HEAD_DIM), F32)],
        compiler_params=pltpu.CompilerParams(dimension_semantics=("arbitrary", "arbitrary", "arbitrary"), vmem_limit_bytes=vm),
        name="moba_prompt",
    )(proj, proj, proj)


KM_BLOCKS_PER_STEP = 8
KM_PAGES_PER_STEP = KM_BLOCKS_PER_STEP * PAGES_PER_BLOCK


def _kmean_kernel(pt_ref, *refs):
    del pt_ref
    page_refs, o_ref = refs[:-1], refs[-1]
    for n in range(KM_BLOCKS_PER_STEP):
        s = jnp.zeros((1, MOBA_WIDTH), F32)
        for p in range(PAGES_PER_BLOCK):
            s = s + jnp.sum(page_refs[n * PAGES_PER_BLOCK + p][...], axis=0, keepdims=True)
        o_ref[n:n + 1, :] = s / MOBA_BLOCK


def _cache_kmean(page_table, cache_k2):
    bsz, n_pages = page_table.shape
    steps = n_pages // KM_PAGES_PER_STEP
    nblk = n_pages // PAGES_PER_BLOCK

    def page_spec(p):
        return pl.BlockSpec((None, None, PAGE_SIZE, MOBA_WIDTH),
                            lambda l, b, g, pt: (l, pt[b * n_pages + g * KM_PAGES_PER_STEP + p], 0, 0))

    grid_spec = pltpu.PrefetchScalarGridSpec(
        num_scalar_prefetch=1,
        grid=(DEPTH, bsz, steps),
        in_specs=[page_spec(p) for p in range(KM_PAGES_PER_STEP)],
        out_specs=pl.BlockSpec((None, None, KM_BLOCKS_PER_STEP, MOBA_WIDTH), lambda l, b, g, pt: (l, b, g, 0)),
    )
    return pl.pallas_call(
        _kmean_kernel,
        grid_spec=grid_spec,
        out_shape=jax.ShapeDtypeStruct((DEPTH, bsz, nblk, MOBA_WIDTH), F32),
        compiler_params=pltpu.CompilerParams(
            dimension_semantics=("arbitrary", "arbitrary", "arbitrary"),
            vmem_limit_bytes=_vmem_limit(2 * KM_PAGES_PER_STEP * _nbytes((PAGE_SIZE, MOBA_WIDTH), F32))),
        name="cache_kmean",
    )(page_table.reshape(-1), *([cache_k2] * KM_PAGES_PER_STEP))


def _select_kernel(q_ref, km_ref, o_ref, *, nblk):
    rows = q_ref.shape[0]
    lane = lax.broadcasted_iota(jnp.int32, (rows, nblk), 1).astype(F32)
    s = _dot_nt(q_ref[...], km_ref[...], precision=HIGHEST)
    out_lane = lax.broadcasted_iota(jnp.int32, (rows, LANES), 1)
    out = jnp.zeros((rows, LANES), F32)
    for j in range(MOBA_TOPK):
        m = jnp.max(s, axis=1, keepdims=True)
        first = jnp.min(jnp.where(s == m, lane, float(nblk)), axis=1, keepdims=True)
        out = jnp.where(out_lane == j, first, out)
        s = jnp.where(lane == first, -jnp.inf, s)
    o_ref[...] = out.astype(jnp.int32)


def _sample_select(proj, kmean_l, *, seqs, seq_len):
    nblk = kmean_l.shape[1]
    return pl.pallas_call(
        functools.partial(_select_kernel, nblk=nblk),
        grid=(seqs, MOBA_HEADS),
        in_specs=[
            pl.BlockSpec((seq_len, HEAD_DIM), lambda b, h: (b, OFF_MQ // HEAD_DIM + h)),
            pl.BlockSpec((None, nblk, HEAD_DIM), lambda b, h: (b, 0, h)),
        ],
        out_specs=pl.BlockSpec((None, None, seq_len, LANES), lambda b, h: (b, h, 0, 0)),
        out_shape=jax.ShapeDtypeStruct((seqs, MOBA_HEADS, seq_len, LANES), jnp.int32),
        compiler_params=pltpu.CompilerParams(dimension_semantics=("arbitrary", "arbitrary")),
        name="sample_select",
    )(proj, kmean_l)


def _sample_attn_kernel(idx_ref, pt_ref, q_ref, kn_ref, vn_ref, ck_hbm, cv_hbm, o_ref, kbuf, vbuf, sem, *, layer, seq_len, n_pages):
    b, h = pl.program_id(0), pl.program_id(1)
    nsel = seq_len * MOBA_TOPK
    npg = nsel * PAGES_PER_BLOCK
    lanes = pl.ds(pl.multiple_of(h * HEAD_DIM, HEAD_DIM), HEAD_DIM)

    def copies(i):
        blk = idx_ref[(b * MOBA_HEADS + h) * nsel + i // PAGES_PER_BLOCK]
        pg = pt_ref[b * n_pages + blk * PAGES_PER_BLOCK + i % PAGES_PER_BLOCK]
        return (pltpu.make_async_copy(ck_hbm.at[layer, pg, :, lanes], kbuf.at[i], sem.at[0]),
                pltpu.make_async_copy(cv_hbm.at[layer, pg, :, lanes], vbuf.at[i], sem.at[1]))

    for i in range(npg):
        ck, cv = copies(i)
        ck.start()
        cv.start()
    for i in range(npg):
        ck, cv = copies(i)
        ck.wait()
        cv.wait()

    q = q_ref[...]
    scale = HEAD_DIM ** -0.5
    nkeys = npg * PAGE_SIZE
    per_q = MOBA_TOPK * MOBA_BLOCK
    kall = kbuf[...].reshape(nkeys, HEAD_DIM)
    vall = vbuf[...].reshape(nkeys, HEAD_DIM)
    logits = _dot_nt(q, kall) * scale
    row = lax.broadcasted_iota(jnp.int32, (seq_len, nkeys), 0)
    col = lax.broadcasted_iota(jnp.int32, (seq_len, nkeys), 1)
    allowed = (col >= row * per_q) & (col < (row + 1) * per_q)
    own = _dot_nt(q, kn_ref[...]) * scale
    orow = lax.broadcasted_iota(jnp.int32, (seq_len, seq_len), 0)
    ocol = lax.broadcasted_iota(jnp.int32, (seq_len, seq_len), 1)
    oallowed = ocol <= orow
    m = jnp.maximum(jnp.max(jnp.where(allowed, logits, MASK_VALUE), axis=1, keepdims=True),
                    jnp.max(jnp.where(oallowed, own, MASK_VALUE), axis=1, keepdims=True))
    p = jnp.where(allowed, jnp.exp(logits - m), 0.0)
    po = jnp.where(oallowed, jnp.exp(own - m), 0.0)
    l = jnp.sum(p, axis=1, keepdims=True) + jnp.sum(po, axis=1, keepdims=True)
    acc = jnp.dot(p, vall, preferred_element_type=F32) + jnp.dot(po, vn_ref[...], preferred_element_type=F32)
    o_ref[...] = (acc / l).astype(o_ref.dtype)


def _sample_attn(idx, page_table, proj, cache_k2, cache_v2, *, layer, seqs, seq_len):
    n = proj.shape[0]
    n_pages = page_table.shape[1]
    npg = seq_len * MOBA_TOPK * PAGES_PER_BLOCK
    hb = lambda off: off // HEAD_DIM
    grid_spec = pltpu.PrefetchScalarGridSpec(
        num_scalar_prefetch=2,
        grid=(seqs, MOBA_HEADS),
        in_specs=[
            pl.BlockSpec((seq_len, HEAD_DIM), lambda b, h, ix, pt: (b, hb(OFF_MQ) + h)),
            pl.BlockSpec((seq_len, HEAD_DIM), lambda b, h, ix, pt: (b, hb(OFF_MK) + h)),
            pl.BlockSpec((seq_len, HEAD_DIM), lambda b, h, ix, pt: (b, hb(OFF_MV) + h)),
            pl.BlockSpec(memory_space=pl.ANY),
            pl.BlockSpec(memory_space=pl.ANY),
        ],
        out_specs=pl.BlockSpec((seq_len, HEAD_DIM), lambda b, h, ix, pt: (b, h)),
        scratch_shapes=[pltpu.VMEM((npg, PAGE_SIZE, HEAD_DIM), F32), pltpu.VMEM((npg, PAGE_SIZE, HEAD_DIM), F32),
                        pltpu.SemaphoreType.DMA((2,))],
    )
    vm = _vmem_limit(8 * npg * _nbytes((PAGE_SIZE, HEAD_DIM), F32))
    return pl.pallas_call(
        functools.partial(_sample_attn_kernel, layer=layer, seq_len=seq_len, n_pages=n_pages),
        grid_spec=grid_spec,
        out_shape=jax.ShapeDtypeStruct((n, MOBA_WIDTH), F32),
        compiler_params=pltpu.CompilerParams(dimension_semantics=("arbitrary", "arbitrary"), vmem_limit_bytes=vm),
        name="sample_attn",
    )(idx, page_table.reshape(-1), proj, proj, proj, cache_k2, cache_v2)


def _outproj_kernel(og_ref, om_ref, oc_ref, w_ref, x_ref, gt_ref, g_ref, o_ref):
    acc = jnp.dot(og_ref[...].astype(BF16), w_ref[0:GLA_WIDTH, :], preferred_element_type=F32)
    acc = acc + jnp.dot(om_ref[...].astype(BF16), w_ref[GLA_WIDTH:GLA_WIDTH + MOBA_WIDTH, :], preferred_element_type=F32)
    acc = acc + jnp.dot(oc_ref[...].astype(BF16), w_ref[GLA_WIDTH + MOBA_WIDTH:D_MODEL, :], preferred_element_type=F32)
    o_ref[...] = x_ref[...] + gt_ref[...] * _rms(acc, g_ref[...])


def _outproj(og, om, oc, w, x, gt, g, *, tm, mod_spec):
    n = x.shape[0]
    mod1 = pl.BlockSpec(mod_spec.block_shape, lambda i: mod_spec.index_map(i, 0))
    vm = _vmem_limit(2 * _nbytes((D_MODEL, D_MODEL), BF16), 2 * _nbytes((tm, D_MODEL), BF16), 6 * _nbytes((tm, D_MODEL), F32))
    return pl.pallas_call(
        _outproj_kernel,
        grid=(n // tm,),
        in_specs=[
            pl.BlockSpec((tm, GLA_WIDTH), lambda i: (i, 0)),
            pl.BlockSpec((tm, MOBA_WIDTH), lambda i: (i, 0)),
            pl.BlockSpec((tm, CONV_CH), lambda i: (i, 0)),
            pl.BlockSpec((D_MODEL, D_MODEL), lambda i: (0, 0)),
            pl.BlockSpec((tm, D_MODEL), lambda i: (i, 0)),
            mod1,
            pl.BlockSpec((1, D_MODEL), lambda i: (0, 0)),
        ],
        out_specs=pl.BlockSpec((tm, D_MODEL), lambda i: (i, 0)),
        out_shape=jax.ShapeDtypeStruct((n, D_MODEL), F32),
        compiler_params=pltpu.CompilerParams(dimension_semantics=("arbitrary",), vmem_limit_bytes=vm),
        name="outproj",
    )(og, om, oc, w, x, gt, g)


FFN_TF = 512


def _ffn_kernel(x_ref, sh_ref, sc_ref, gt_ref, gpre_ref, gpost_ref, w1_ref, w3_ref, w2_ref, o_ref, h_scr, acc_scr):
    j = pl.program_id(1)

    @pl.when(j == 0)
    def _():
        h = _rms(x_ref[...], gpre_ref[...]) * (1.0 + sc_ref[...]) + sh_ref[...]
        h_scr[...] = h.astype(BF16)
        acc_scr[...] = jnp.zeros_like(acc_scr)

    h = h_scr[...]
    u = jnp.dot(h, w1_ref[...], preferred_element_type=F32)
    g = jnp.dot(h, w3_ref[...], preferred_element_type=F32)
    a = (_silu(u) * g).astype(BF16)
    acc_scr[...] += jnp.dot(a, w2_ref[...], preferred_element_type=F32)

    @pl.when(j == pl.num_programs(1) - 1)
    def _():
        o_ref[...] = x_ref[...] + gt_ref[...] * _rms(acc_scr[...], gpost_ref[...])


def _ffn(x, sh, sc, gt, gpre, gpost, w1, w3, w2, *, tm, mod_spec):
    n = x.shape[0]
    vec = pl.BlockSpec((1, D_MODEL), lambda i, j: (0, 0))
    vm = _vmem_limit(4 * _nbytes((tm, D_MODEL), F32), _nbytes((tm, D_MODEL), BF16), _nbytes((tm, D_MODEL), F32),
                     6 * _nbytes((D_MODEL, FFN_TF), BF16), 3 * _nbytes((tm, FFN_TF), F32))
    return pl.pallas_call(
        _ffn_kernel,
        grid=(n // tm, D_FF // FFN_TF),
        in_specs=[
            pl.BlockSpec((tm, D_MODEL), lambda i, j: (i, 0)),
            mod_spec, mod_spec, mod_spec, vec, vec,
            pl.BlockSpec((D_MODEL, FFN_TF), lambda i, j: (0, j)),
            pl.BlockSpec((D_MODEL, FFN_TF), lambda i, j: (0, j)),
            pl.BlockSpec((FFN_TF, D_MODEL), lambda i, j: (j, 0)),
        ],
        out_specs=pl.BlockSpec((tm, D_MODEL), lambda i, j: (i, 0)),
        out_shape=jax.ShapeDtypeStruct((n, D_MODEL), F32),
        scratch_shapes=[pltpu.VMEM((tm, D_MODEL), BF16), pltpu.VMEM((tm, D_MODEL), F32)],
        compiler_params=pltpu.CompilerParams(dimension_semantics=("arbitrary", "arbitrary"), vmem_limit_bytes=vm),
        name="ffn",
    )(x, sh, sc, gt, gpre, gpost, w1, w3, w2)


def _rope_tables(pos):
    inv = ROPE_THETA ** (-jnp.arange(0, HEAD_DIM, 2, dtype=F32) / HEAD_DIM)
    ang = pos.astype(F32)[:, None] * inv[None, :]
    cos, sin = jnp.cos(ang), jnp.sin(ang)
    return jnp.concatenate([cos, cos], axis=-1), jnp.concatenate([-sin, sin], axis=-1)


PROMPT_TM = 512
PROMPT_TB = 256


def kernel(x_prompt, x_sample, c_prompt, c_sample, cache_k, cache_v, state_gla, state_conv, page_table, w_ada, b_ada, g_pre_mix, g_post_mix, g_pre_ffn, g_post_ffn, w_in, w_a2, b_a, gla_norm_g, conv_w, conv_b, conv_ln_g, conv_ln_b, w_out, w_ffn1, w_ffn3, w_ffn2):
    bp, tp, _ = x_prompt.shape
    bs, ts, _ = x_sample.shape
    n_phys = cache_k.shape[1]
    past = page_table.shape[1] * cache_k.shape[2]
    np_, ns = bp * tp, bs * ts
    assert tp % PROMPT_TM == 0 and tp % MOBA_BLOCK == 0 and past % MOBA_BLOCK == 0 and ts <= CONV_WIDTH - 1

    w_main = jnp.concatenate([w_in[:, :, :GATE_COL0], w_in[:, :, GATE_COL0 + GLA_GATE_RANK:]], axis=-1).astype(BF16)
    w_ga = jnp.pad(w_in[:, :, GATE_COL0:GATE_COL0 + GLA_GATE_RANK], ((0, 0), (0, 0), (0, LANES - GLA_GATE_RANK))).astype(BF16)
    wa2 = jnp.pad(w_a2, ((0, 0), (0, LANES - GLA_GATE_RANK), (0, 0)))
    w_out_b, w1_b, w3_b, w2_b = (w.astype(BF16) for w in (w_out, w_ffn1, w_ffn3, w_ffn2))

    rows = -(-(bp + bs) // SUBLANES) * SUBLANES
    c_all = jnp.pad(jnp.concatenate([c_prompt, c_sample], axis=0), ((0, rows - bp - bs), (0, 0)))
    mod = _modulation(c_all, w_ada, b_ada).reshape(DEPTH, rows, 6, D_MODEL)
    mod_p = mod[:, :bp].reshape(DEPTH, bp, 6, 1, D_MODEL)
    mod_s = jnp.repeat(mod[:, bp:bp + bs], ts, axis=1).transpose(0, 2, 1, 3)

    tiles_per_seq = tp // PROMPT_TM
    mod_spec_p = pl.BlockSpec((None, 1, D_MODEL), lambda i, j: (i // tiles_per_seq, 0, 0))
    mod_spec_s = pl.BlockSpec((ns, D_MODEL), lambda i, j: (0, 0))
    cos_p, sin_p = _rope_tables(jnp.arange(tp))
    cos_s, sin_s = _rope_tables(past + jnp.arange(ns) % ts)
    rope_spec_p = pl.BlockSpec((PROMPT_TM, HEAD_DIM), lambda i, j: (i % tiles_per_seq, 0))
    rope_spec_s = pl.BlockSpec((ns, HEAD_DIM), lambda i, j: (0, 0))

    cache_k2 = cache_k.reshape(DEPTH, n_phys, PAGE_SIZE, MOBA_WIDTH)
    cache_v2 = cache_v.reshape(DEPTH, n_phys, PAGE_SIZE, MOBA_WIDTH)
    kmean_past = _cache_kmean(page_table, cache_k2)

    gla0 = jnp.zeros((bp, GLA_HEADS, GLA_DK, GLA_DV), F32)
    buf0 = jnp.zeros((bp, CONV_WIDTH - 1, CONV_CH), F32)

    def vec(a):
        return a.reshape(1, -1)

    xp = x_prompt.reshape(np_, D_MODEL)
    xs = x_sample.reshape(ns, D_MODEL)
    outs = {k: [] for k in ("kp", "vp", "sp", "cp", "ks", "vs", "ss", "cs")}
    for l in range(DEPTH):
        mp = [mod_p[l, :, k] for k in range(6)]
        proj, ga = _inproj(xp, mp[0], mp[1], vec(g_pre_mix[l]), w_main[l], w_ga[l], cos_p, sin_p,
                           tm=PROMPT_TM, mod_spec=mod_spec_p, rope_spec=rope_spec_p)
        og, st = _gla(proj, ga, wa2[l], vec(b_a[l]), vec(gla_norm_g[l]), gla0, seqs=bp, seq_len=tp, tb=PROMPT_TB, chunk=GLA_CHUNK, act_dtype=BF16)
        oc, nb = _conv(proj, conv_w[l], vec(conv_b[l]), vec(conv_ln_g[l]), vec(conv_ln_b[l]), buf0, seqs=bp, seq_len=tp, tb=PROMPT_TB, act_dtype=BF16)
        om = _moba_prompt(proj, seqs=bp, seq_len=tp)
        xp = _outproj(og, om, oc, w_out_b[l], xp, mp[2], vec(g_post_mix[l]), tm=PROMPT_TM, mod_spec=mod_spec_p)
        xp = _ffn(xp, mp[3], mp[4], mp[5], vec(g_pre_ffn[l]), vec(g_post_ffn[l]), w1_b[l], w3_b[l], w2_b[l],
                  tm=PROMPT_TM, mod_spec=mod_spec_p)
        outs["kp"].append(proj[:, OFF_MK:OFF_MK + MOBA_WIDTH])
        outs["vp"].append(proj[:, OFF_MV:OFF_MV + MOBA_WIDTH])
        outs["sp"].append(st)
        outs["cp"].append(nb)

        ms = [mod_s[l, k] for k in range(6)]
        proj, ga = _inproj(xs, ms[0], ms[1], vec(g_pre_mix[l]), w_main[l], w_ga[l], cos_s, sin_s,
                           tm=ns, mod_spec=mod_spec_s, rope_spec=rope_spec_s)
        og, st = _gla(proj, ga, wa2[l], vec(b_a[l]), vec(gla_norm_g[l]), state_gla[l], seqs=bs, seq_len=ts, tb=ts, chunk=ts, act_dtype=F32)
        oc, nb = _conv(proj, conv_w[l], vec(conv_b[l]), vec(conv_ln_g[l]), vec(conv_ln_b[l]), state_conv[l], seqs=bs, seq_len=ts, tb=ts, act_dtype=F32)
        sel = _sample_select(proj, kmean_past[l], seqs=bs, seq_len=ts)
        idx = sel[:, :, :, :MOBA_TOPK].reshape(-1)
        om = _sample_attn(idx, page_table, proj, cache_k2, cache_v2, layer=l, seqs=bs, seq_len=ts)
        xs = _outproj(og, om, oc, w_out_b[l], xs, ms[2], vec(g_post_mix[l]), tm=ns, mod_spec=mod_spec_s)
        xs = _ffn(xs, ms[3], ms[4], ms[5], vec(g_pre_ffn[l]), vec(g_post_ffn[l]), w1_b[l], w3_b[l], w2_b[l],
                  tm=ns, mod_spec=mod_spec_s)
        outs["ks"].append(proj[:, OFF_MK:OFF_MK + MOBA_WIDTH])
        outs["vs"].append(proj[:, OFF_MV:OFF_MV + MOBA_WIDTH])
        outs["ss"].append(st)
        outs["cs"].append(nb)

    kv_p = (DEPTH, bp, tp, MOBA_HEADS, HEAD_DIM)
    kv_s = (DEPTH, bs, ts, MOBA_HEADS, HEAD_DIM)
    return (xp.reshape(bp, tp, D_MODEL), xs.reshape(bs, ts, D_MODEL),
            jnp.stack(outs["kp"]).reshape(kv_p), jnp.stack(outs["vp"]).reshape(kv_p),
            jnp.stack(outs["sp"]), jnp.stack(outs["cp"]),
            jnp.stack(outs["ks"]).reshape(kv_s), jnp.stack(outs["vs"]).reshape(kv_s),
            jnp.stack(outs["ss"]), jnp.stack(outs["cs"]))
```

```python
import functools
import math

import jax
import jax.numpy as jnp
from jax import lax
from jax.experimental import pallas as pl
from jax.experimental.pallas import tpu as pltpu

F32 = jnp.float32
BF16 = jnp.bfloat16
HIGHEST = lax.Precision.HIGHEST

D_MODEL = 2048
DEPTH = 4
GLA_HEADS = 4
GLA_DK = 64
GLA_DV = 128
GLA_QK = GLA_HEADS * GLA_DK
GLA_WIDTH = GLA_HEADS * GLA_DV
GLA_GATE_RANK = 16
GLA_TAU = 16.0
MOBA_HEADS = 8
HEAD_DIM = 128
MOBA_WIDTH = MOBA_HEADS * HEAD_DIM
MOBA_BLOCK = 256
MOBA_TOPK = 3
ROPE_THETA = 10000.0
CONV_CH = 512
CONV_WIDTH = 31
D_FF = 5632
RMS_EPS = 1e-6
LN_EPS = 1e-5
PAGE_SIZE = 128
PAGES_PER_BLOCK = MOBA_BLOCK // PAGE_SIZE

LANES = 128
SUBLANES = 8
VMEM_LIMIT_CAP = 60 * 1024 * 1024

PROJ_COLS = 2 * GLA_QK + 2 * GLA_WIDTH + 3 * MOBA_WIDTH + 2 * CONV_CH
OFF_GQ, OFF_GK, OFF_GV, OFF_GG = 0, 256, 512, 1024
OFF_MQ, OFF_MK, OFF_MV = 1536, 2560, 3584
OFF_CU, OFF_CG = 4608, 5120
GATE_COL0 = 2 * GLA_QK + 2 * GLA_WIDTH
PROJ_TN = 512
MASK_VALUE = -1e30

GLA_CHUNK = 16


def _vmem_limit(*nbytes):
    need = int(sum(nbytes) * 1.25) + (4 << 20)
    return min(max(need, 16 << 20), VMEM_LIMIT_CAP)


def _nbytes(shape, dtype):
    return math.prod(shape) * jnp.dtype(dtype).itemsize


def _sigmoid(x):
    return 1.0 / (1.0 + jnp.exp(-x))


def _silu(x):
    return x * _sigmoid(x)


def _rms(x, g):
    return x * lax.rsqrt(jnp.mean(x * x, axis=-1, keepdims=True) + RMS_EPS) * g


def _dot_nt(a, b, **kw):
    return lax.dot_general(a, b, (((1,), (1,)), ((), ())), preferred_element_type=F32, **kw)


def _dot_tn(a, b, **kw):
    return lax.dot_general(a, b, (((0,), (0,)), ((), ())), preferred_element_type=F32, **kw)


MOD_TN = 1024


def _mod_kernel(c_ref, w_ref, b_ref, o_ref):
    a = _silu(c_ref[...]).astype(BF16)
    o_ref[...] = jnp.dot(a, w_ref[...].astype(BF16), preferred_element_type=F32) + b_ref[...]


def _modulation(c_all, w_ada, b_ada):
    rows = c_all.shape[0]
    n = w_ada.shape[-1]
    return pl.pallas_call(
        _mod_kernel,
        grid=(DEPTH, n // MOD_TN),
        in_specs=[
            pl.BlockSpec((rows, D_MODEL), lambda l, j: (0, 0)),
            pl.BlockSpec((None, D_MODEL, MOD_TN), lambda l, j: (l, 0, j)),
            pl.BlockSpec((None, 1, MOD_TN), lambda l, j: (l, 0, j)),
        ],
        out_specs=pl.BlockSpec((None, rows, MOD_TN), lambda l, j: (l, 0, j)),
        out_shape=jax.ShapeDtypeStruct((DEPTH, rows, n), F32),
        compiler_params=pltpu.CompilerParams(
            dimension_semantics=("arbitrary", "arbitrary"),
            vmem_limit_bytes=_vmem_limit(2 * _nbytes((D_MODEL, MOD_TN), F32), _nbytes((D_MODEL, MOD_TN), BF16))),
        name="adaln_mod",
    )(c_all, w_ada, b_ada.reshape(DEPTH, 1, n))


HEADS_PER_TILE = PROJ_TN // HEAD_DIM


def _inproj_kernel(x_ref, sh_ref, sc_ref, g_ref, w_ref, wga_ref, cos_ref, sin_ref, kin_ref, vin_ref,
                   o_ref, ga_ref, ko_ref, vo_ref, h_scr):
    del kin_ref, vin_ref
    j = pl.program_id(1)

    @pl.when(j == 0)
    def _():
        h = _rms(x_ref[...], g_ref[...]) * (1.0 + sc_ref[...]) + sh_ref[...]
        hb = h.astype(BF16)
        h_scr[...] = hb
        ga_ref[...] = jnp.dot(hb, wga_ref[...], preferred_element_type=F32)

    acc = jnp.dot(h_scr[...], w_ref[...], preferred_element_type=F32)

    def slabs(rot):
        for s in range(HEADS_PER_TILE):
            a = acc[:, s * HEAD_DIM:(s + 1) * HEAD_DIM]
            if rot:
                a = a * cos_ref[...] + pltpu.roll(a, HEAD_DIM // 2, 1) * sin_ref[...]
            yield s, a

    for jj in range(PROJ_COLS // PROJ_TN):
        col = jj * PROJ_TN
        rot = OFF_MQ <= col < OFF_MV
        head_out = ko_ref if OFF_MK <= col < OFF_MV else vo_ref if OFF_MV <= col < OFF_CU else None
        head0 = (col - (OFF_MK if head_out is ko_ref else OFF_MV)) // HEAD_DIM

        @pl.when(j == jj)
        def _(rot=rot, head_out=head_out, head0=head0):
            if not rot and head_out is None:
                o_ref[...] = acc
                return
            for s, a in slabs(rot):
                o_ref[:, s * HEAD_DIM:(s + 1) * HEAD_DIM] = a
                if head_out is not None:
                    head_out[:, head0 + s, :] = a


def _inproj(x, sh, sc, g, w_main, w_ga, cos2, sin2, kv_bufs, *, layer, tm, mod_spec, rope_spec):
    n = x.shape[0]
    vm = _vmem_limit(2 * _nbytes((tm, D_MODEL), F32), _nbytes((tm, D_MODEL), BF16), 2 * _nbytes((D_MODEL, PROJ_TN), BF16),
                     2 * _nbytes((tm, PROJ_TN), F32), 2 * _nbytes((D_MODEL, LANES), BF16), 6 * _nbytes((tm, LANES), F32),
                     4 * _nbytes((tm, MOBA_WIDTH), F32),
                     4 * _nbytes(mod_spec.block_shape[-2:], F32) if mod_spec.block_shape[-2] != 1 else 0)
    kv_spec = pl.BlockSpec((None, tm, MOBA_HEADS, HEAD_DIM), lambda i, j: (layer, i, 0, 0))
    kv_shape = jax.ShapeDtypeStruct((DEPTH, n, MOBA_HEADS, HEAD_DIM), F32)
    n_in = 8
    return pl.pallas_call(
        _inproj_kernel,
        grid=(n // tm, PROJ_COLS // PROJ_TN),
        in_specs=[
            pl.BlockSpec((tm, D_MODEL), lambda i, j: (i, 0)),
            mod_spec, mod_spec,
            pl.BlockSpec((1, D_MODEL), lambda i, j: (0, 0)),
            pl.BlockSpec((D_MODEL, PROJ_TN), lambda i, j: (0, j)),
            pl.BlockSpec((D_MODEL, LANES), lambda i, j: (0, 0)),
            rope_spec, rope_spec,
            pl.BlockSpec(memory_space=pl.ANY), pl.BlockSpec(memory_space=pl.ANY),
        ],
        out_specs=[
            pl.BlockSpec((tm, PROJ_TN), lambda i, j: (i, j)),
            pl.BlockSpec((tm, LANES), lambda i, j: (i, 0)),
            kv_spec, kv_spec,
        ],
        out_shape=[jax.ShapeDtypeStruct((n, PROJ_COLS), F32), jax.ShapeDtypeStruct((n, LANES), F32), kv_shape, kv_shape],
        scratch_shapes=[pltpu.VMEM((tm, D_MODEL), BF16)],
        input_output_aliases={n_in: 2, n_in + 1: 3},
        compiler_params=pltpu.CompilerParams(dimension_semantics=("arbitrary", "arbitrary"), vmem_limit_bytes=vm),
        name="inproj",
    )(x, sh, sc, g, w_main, w_ga, cos2, sin2, *kv_bufs)


GLA_PAD = 16


def _gla_kernel(q_ref, k_ref, v_ref, gg_ref, ga_ref, wa2_ref, ba_ref, gn_ref, s0_ref, hm_ref, o_ref, st_ref,
                st_scr, b_scr, k_scr, v_scr, o_scr, *, tb, chunk):
    t = pl.program_id(1)
    pad = GLA_PAD

    @pl.when(t == 0)
    def _():
        for h in range(GLA_HEADS):
            st_scr[h] = s0_ref[h].T
        b_scr[0:pad, :] = jnp.zeros((pad, GLA_QK), F32)
        k_scr[0:pad, :] = jnp.zeros((pad, GLA_QK), F32)
        v_scr[0:pad, :] = jnp.zeros((pad, GLA_WIDTH), F32)

    xg = jnp.dot(ga_ref[...], wa2_ref[...], preferred_element_type=F32, precision=HIGHEST) + ba_ref[...]
    la = -(jnp.maximum(-xg, 0.0) + jnp.log1p(jnp.exp(-jnp.abs(xg)))) / GLA_TAU
    rowc = lax.broadcasted_iota(jnp.int32, (tb, GLA_QK), 0) & (chunk - 1)
    cur = pl.ds(pad, tb)

    b_scr[cur, :] = la
    step = 1
    while step < chunk:
        b_scr[cur, :] = b_scr[cur, :] + jnp.where(rowc >= step, b_scr[pl.ds(pad - step, tb), :], 0.0)
        step *= 2
    b = b_scr[cur, :]
    k = k_ref[...]
    k_scr[cur, :] = k
    v_scr[cur, :] = v_ref[...]
    qs = q_ref[...] * (GLA_DK ** -0.5)

    acc = jnp.zeros((tb, GLA_WIDTH), F32)
    for d in range(chunk):
        sh = pl.ds(pad - d, tb)
        e = jnp.exp(jnp.minimum(b - b_scr[sh, :], 0.0))
        term = jnp.where(rowc >= d, qs * k_scr[sh, :] * e, 0.0)
        acc = acc + jnp.dot(term.astype(BF16), hm_ref[...], preferred_element_type=F32) * v_scr[sh, :]
    o_scr[...] = acc

    eb = jnp.exp(b)
    states = [st_scr[h] for h in range(GLA_HEADS)]
    for n in range(tb // chunk):
        r0 = n * chunk
        bn = b[r0:r0 + chunk, :]
        bl = bn[chunk - 1:chunk, :]
        qd = qs[r0:r0 + chunk, :] * eb[r0:r0 + chunk, :]
        kd = k[r0:r0 + chunk, :] * jnp.exp(bl - bn)
        ebl = eb[r0 + chunk - 1:r0 + chunk, :]
        for h in range(GLA_HEADS):
            dsl = slice(h * GLA_DK, (h + 1) * GLA_DK)
            vsl = slice(h * GLA_DV, (h + 1) * GLA_DV)
            o_scr[r0:r0 + chunk, vsl] += _dot_nt(qd[:, dsl], states[h])
            states[h] = ebl[:, dsl] * states[h] + _dot_tn(v_ref[r0:r0 + chunk, vsl], kd[:, dsl])
    for h in range(GLA_HEADS):
        st_scr[h] = states[h]

    gn = gn_ref[...]
    for h in range(GLA_HEADS):
        vsl = slice(h * GLA_DV, (h + 1) * GLA_DV)
        o_ref[:, vsl] = (_rms(o_scr[:, vsl], gn) * _silu(gg_ref[:, vsl])).astype(o_ref.dtype)

    @pl.when(t == pl.num_programs(1) - 1)
    def _():
        for h in range(GLA_HEADS):
            st_ref[h] = states[h].T


def _gla_head_matrix():
    r = lax.broadcasted_iota(jnp.int32, (GLA_QK, GLA_WIDTH), 0) // GLA_DK
    c = lax.broadcasted_iota(jnp.int32, (GLA_QK, GLA_WIDTH), 1) // GLA_DV
    return (r == c).astype(BF16)


def _gla(proj, ga, wa2, ba, gn, s0, *, seqs, seq_len, tb, act_dtype):
    n = proj.shape[0]
    steps = seq_len // tb
    chunk = min(GLA_CHUNK, tb)
    assert chunk & (chunk - 1) == 0 and chunk - 1 <= GLA_PAD and tb % chunk == 0
    row = lambda b, t: b * steps + t
    kern = functools.partial(_gla_kernel, tb=tb, chunk=chunk)
    return pl.pallas_call(
        kern,
        grid=(seqs, steps),
        in_specs=[
            pl.BlockSpec((tb, GLA_QK), lambda b, t: (row(b, t), OFF_GQ // GLA_QK)),
            pl.BlockSpec((tb, GLA_QK), lambda b, t: (row(b, t), OFF_GK // GLA_QK)),
            pl.BlockSpec((tb, GLA_WIDTH), lambda b, t: (row(b, t), OFF_GV // GLA_WIDTH)),
            pl.BlockSpec((tb, GLA_WIDTH), lambda b, t: (row(b, t), OFF_GG // GLA_WIDTH)),
            pl.BlockSpec((tb, LANES), lambda b, t: (row(b, t), 0)),
            pl.BlockSpec((LANES, GLA_QK), lambda b, t: (0, 0)),
            pl.BlockSpec((1, GLA_QK), lambda b, t: (0, 0)),
            pl.BlockSpec((1, GLA_DV), lambda b, t: (0, 0)),
            pl.BlockSpec((None, GLA_HEADS, GLA_DK, GLA_DV), lambda b, t: (b, 0, 0, 0)),
            pl.BlockSpec((GLA_QK, GLA_WIDTH), lambda b, t: (0, 0)),
        ],
        out_specs=[
            pl.BlockSpec((tb, GLA_WIDTH), lambda b, t: (row(b, t), 0)),
            pl.BlockSpec((None, GLA_HEADS, GLA_DK, GLA_DV), lambda b, t: (b, 0, 0, 0)),
        ],
        out_shape=[jax.ShapeDtypeStruct((n, GLA_WIDTH), act_dtype),
                   jax.ShapeDtypeStruct((seqs, GLA_HEADS, GLA_DK, GLA_DV), F32)],
        scratch_shapes=[pltpu.VMEM((GLA_HEADS, GLA_DV, GLA_DK), F32),
                        pltpu.VMEM((GLA_PAD + tb, GLA_QK), F32), pltpu.VMEM((GLA_PAD + tb, GLA_QK), F32),
                        pltpu.VMEM((GLA_PAD + tb, GLA_WIDTH), F32), pltpu.VMEM((tb, GLA_WIDTH), F32)],
        compiler_params=pltpu.CompilerParams(dimension_semantics=("arbitrary", "arbitrary")),
        name="gla",
    )(proj, proj, proj, proj, ga, wa2, ba, gn, s0, _gla_head_matrix())


HALO = 32


def _conv_kernel(cu_ref, cg_ref, cw_ref, cb_ref, lg_ref, lb_ref, buf_ref, o_ref, nb_ref, ext, *, tb):
    t = pl.program_id(1)
    hist = CONV_WIDTH - 1

    @pl.when(t == 0)
    def _():
        ext[HALO - hist:HALO, :] = buf_ref[...]

    @pl.when(t > 0)
    def _():
        ext[0:HALO, :] = ext[tb:tb + HALO, :]

    ext[HALO:HALO + tb, :] = cu_ref[...] * _sigmoid(cg_ref[...])
    y = jnp.broadcast_to(cb_ref[...], (tb, CONV_CH))
    for w in range(CONV_WIDTH):
        y = y + ext[HALO - hist + w:HALO - hist + w + tb, :] * cw_ref[w:w + 1, :]
    nb_ref[...] = ext[HALO + tb - hist:HALO + tb, :]
    mu = jnp.mean(y, axis=-1, keepdims=True)
    yc = y - mu
    var = jnp.mean(yc * yc, axis=-1, keepdims=True)
    yn = yc * lax.rsqrt(var + LN_EPS) * lg_ref[...] + lb_ref[...]
    o_ref[...] = _silu(yn).astype(o_ref.dtype)


def _conv(proj, cw, cb, lg, lb, buf0, *, seqs, seq_len, tb, act_dtype):
    n = proj.shape[0]
    steps = seq_len // tb
    row = lambda b, t: b * steps + t
    hist = CONV_WIDTH - 1
    vec = pl.BlockSpec((1, CONV_CH), lambda b, t: (0, 0))
    return pl.pallas_call(
        functools.partial(_conv_kernel, tb=tb),
        grid=(seqs, steps),
        in_specs=[
            pl.BlockSpec((tb, CONV_CH), lambda b, t: (row(b, t), OFF_CU // CONV_CH)),
            pl.BlockSpec((tb, CONV_CH), lambda b, t: (row(b, t), OFF_CG // CONV_CH)),
            pl.BlockSpec((CONV_WIDTH, CONV_CH), lambda b, t: (0, 0)),
            vec, vec, vec,
            pl.BlockSpec((None, hist, CONV_CH), lambda b, t: (b, 0, 0)),
        ],
        out_specs=[
            pl.BlockSpec((tb, CONV_CH), lambda b, t: (row(b, t), 0)),
            pl.BlockSpec((None, hist, CONV_CH), lambda b, t: (b, 0, 0)),
        ],
        out_shape=[jax.ShapeDtypeStruct((n, CONV_CH), act_dtype), jax.ShapeDtypeStruct((seqs, hist, CONV_CH), F32)],
        scratch_shapes=[pltpu.VMEM((HALO + tb, CONV_CH), F32)],
        compiler_params=pltpu.CompilerParams(dimension_semantics=("arbitrary", "arbitrary")),
        name="conv",
    )(proj, proj, cw, cb, lg, lb, buf0)


def _top_blocks_t(s, n_past, blk_id):
    nblk = float(s.shape[0])
    past = blk_id < n_past
    s = jnp.where(past, s, -jnp.inf)
    sel = jnp.zeros(s.shape, F32)
    for _ in range(MOBA_TOPK):
        m = jnp.max(s, axis=0, keepdims=True)
        first = jnp.min(jnp.where(s == m, blk_id, nblk), axis=0, keepdims=True)
        pick = (blk_id == first) & past
        sel = jnp.where(pick, 1.0, sel)
        s = jnp.where(pick, -jnp.inf, s)
    return sel


MOBA_UNROLL = 4


def _moba_prompt_kernel(q_ref, k_ref, v_ref, o_ref, kmean_scr, kb_scr, vt_scr, l_scr, *, nblk):
    qi = pl.program_id(2)
    blk = MOBA_BLOCK

    @pl.when(qi == 0)
    def _():
        for n in range(nblk):
            kblk = k_ref[n * blk:(n + 1) * blk, :]
            kmean_scr[n:n + 1, :] = jnp.mean(kblk, axis=0, keepdims=True)
            kb_scr[n] = kblk.astype(BF16)
            vt_scr[n] = v_ref[n * blk:(n + 1) * blk, :].T.astype(BF16)

    qt = q_ref[...].T
    blk_id = lax.broadcasted_iota(jnp.int32, (nblk, blk), 0)
    blk_f = blk_id.astype(F32)
    sel = _top_blocks_t(jnp.dot(kmean_scr[...], qt, preferred_element_type=F32, precision=HIGHEST), qi.astype(F32), blk_f)
    qtb = qt.astype(BF16)
    scale = HEAD_DIM ** -0.5
    key = lax.broadcasted_iota(jnp.int32, (blk, blk), 0)
    qry = lax.broadcasted_iota(jnp.int32, (blk, blk), 1)
    l_own = jnp.where(key <= qry, jnp.dot(kb_scr[qi], qtb, preferred_element_type=F32) * scale, MASK_VALUE)

    def fold(x, op):
        return op(x.reshape(blk // SUBLANES, SUBLANES, blk), axis=0)

    def pass1(kj, mmax):
        logits = jnp.dot(kb_scr[kj], qtb, preferred_element_type=F32) * scale
        picked = jnp.max(jnp.where(blk_id == kj, sel, 0.0), axis=0, keepdims=True)
        logits = jnp.where(picked > 0.5, logits, MASK_VALUE)
        l_scr[kj] = logits
        return jnp.maximum(mmax, fold(logits, jnp.max))

    def pass2(kj, carry):
        lsum, acc = carry
        p = jnp.exp(l_scr[kj] - m)
        return lsum + fold(p, jnp.sum), acc + jnp.dot(vt_scr[kj], p.astype(BF16), preferred_element_type=F32)

    def grouped(body, carry):
        def group(g, c):
            for u in range(MOBA_UNROLL):
                c = body(g * MOBA_UNROLL + u, c)
            return c
        return lax.fori_loop(0, (qi + MOBA_UNROLL - 1) // MOBA_UNROLL, group, carry)

    m = jnp.max(grouped(pass1, fold(l_own, jnp.max)), axis=0, keepdims=True)
    p_own = jnp.exp(l_own - m)
    lsum, acc = grouped(pass2, (fold(p_own, jnp.sum), jnp.dot(vt_scr[qi], p_own.astype(BF16), preferred_element_type=F32)))
    o_ref[...] = (acc / jnp.sum(lsum, axis=0, keepdims=True)).T.astype(o_ref.dtype)


def _moba_prompt(proj, *, seqs, seq_len):
    n = proj.shape[0]
    nblk = seq_len // MOBA_BLOCK
    assert nblk % MOBA_UNROLL == 0
    hb = lambda off: off // HEAD_DIM
    vm = _vmem_limit(4 * _nbytes((seq_len, HEAD_DIM), F32), 2 * _nbytes((seq_len, HEAD_DIM), BF16),
                     (nblk + 12) * _nbytes((MOBA_BLOCK, MOBA_BLOCK), F32))
    return pl.pallas_call(
        functools.partial(_moba_prompt_kernel, nblk=nblk),
        grid=(seqs, MOBA_HEADS, nblk),
        in_specs=[
            pl.BlockSpec((MOBA_BLOCK, HEAD_DIM), lambda b, h, i: (b * nblk + i, hb(OFF_MQ) + h)),
            pl.BlockSpec((seq_len, HEAD_DIM), lambda b, h, i: (b, hb(OFF_MK) + h)),
            pl.BlockSpec((seq_len, HEAD_DIM), lambda b, h, i: (b, hb(OFF_MV) + h)),
        ],
        out_specs=pl.BlockSpec((MOBA_BLOCK, HEAD_DIM), lambda b, h, i: (b * nblk + i, h)),
        out_shape=jax.ShapeDtypeStruct((n, MOBA_WIDTH), BF16),
        scratch_shapes=[pltpu.VMEM((nblk, HEAD_DIM), F32), pltpu.VMEM((nblk, MOBA_BLOCK, HEAD_DIM), BF16),
                        pltpu.VMEM((nblk, HEAD_DIM, MOBA_BLOCK), BF16), pltpu.VMEM((nblk, MOBA_BLOCK, MOBA_BLOCK), F32)],
        compiler_params=pltpu.CompilerParams(dimension_semantics=("arbitrary", "arbitrary", "arbitrary"), vmem_limit_bytes=vm),
        name="moba_prompt",
    )(proj, proj, proj)


KM_BLOCKS_PER_STEP = 8
KM_PAGES_PER_STEP = KM_BLOCKS_PER_STEP * PAGES_PER_BLOCK


def _kmean_kernel(pt_ref, *refs):
    del pt_ref
    page_refs, o_ref = refs[:-1], refs[-1]
    for n in range(KM_BLOCKS_PER_STEP):
        s = jnp.zeros((MOBA_HEADS, HEAD_DIM), F32)
        for p in range(PAGES_PER_BLOCK):
            s = s + jnp.sum(page_refs[n * PAGES_PER_BLOCK + p][...], axis=0)
        o_ref[n] = s / MOBA_BLOCK


def _cache_kmean(page_table, cache_k):
    bsz, n_pages = page_table.shape
    steps = n_pages // KM_PAGES_PER_STEP
    nblk = n_pages // PAGES_PER_BLOCK

    def page_spec(p):
        return pl.BlockSpec((None, None, PAGE_SIZE, MOBA_HEADS, HEAD_DIM),
                            lambda l, b, g, pt: (l, pt[b * n_pages + g * KM_PAGES_PER_STEP + p], 0, 0, 0))

    grid_spec = pltpu.PrefetchScalarGridSpec(
        num_scalar_prefetch=1,
        grid=(DEPTH, bsz, steps),
        in_specs=[page_spec(p) for p in range(KM_PAGES_PER_STEP)],
        out_specs=pl.BlockSpec((None, None, KM_BLOCKS_PER_STEP, MOBA_HEADS, HEAD_DIM), lambda l, b, g, pt: (l, b, g, 0, 0)),
    )
    return pl.pallas_call(
        _kmean_kernel,
        grid_spec=grid_spec,
        out_shape=jax.ShapeDtypeStruct((DEPTH, bsz, nblk, MOBA_HEADS, HEAD_DIM), F32),
        compiler_params=pltpu.CompilerParams(
            dimension_semantics=("arbitrary", "arbitrary", "arbitrary"),
            vmem_limit_bytes=_vmem_limit(2 * KM_PAGES_PER_STEP * _nbytes((PAGE_SIZE, MOBA_WIDTH), F32))),
        name="cache_kmean",
    )(page_table.reshape(-1), *([cache_k] * KM_PAGES_PER_STEP))


def _select_kernel(qa_ref, qb_ref, km_ref, o_ref, *, nblk):
    rows = qa_ref.shape[0]
    half = MOBA_HEADS // 2
    lane = lax.broadcasted_iota(jnp.int32, (rows, nblk), 1).astype(F32)
    out_lane = lax.broadcasted_iota(jnp.int32, (rows, LANES), 1)
    for h in range(MOBA_HEADS):
        q_ref = qa_ref if h < half else qb_ref
        q = q_ref[:, (h % half) * HEAD_DIM:(h % half + 1) * HEAD_DIM]
        s = _dot_nt(q, km_ref[:, h, :], precision=HIGHEST)
        out = jnp.zeros((rows, LANES), F32)
        for j in range(MOBA_TOPK):
            m = jnp.max(s, axis=1, keepdims=True)
            first = jnp.min(jnp.where(s == m, lane, float(nblk)), axis=1, keepdims=True)
            out = jnp.where(out_lane == j, first, out)
            s = jnp.where(lane == first, -jnp.inf, s)
        o_ref[h] = out.astype(jnp.int32)


def _sample_select(proj, kmean_l, *, seqs, seq_len):
    nblk = kmean_l.shape[1]
    half_w = MOBA_WIDTH // 2
    return pl.pallas_call(
        functools.partial(_select_kernel, nblk=nblk),
        grid=(seqs,),
        in_specs=[
            pl.BlockSpec((seq_len, half_w), lambda b: (b, OFF_MQ // half_w)),
            pl.BlockSpec((seq_len, half_w), lambda b: (b, OFF_MQ // half_w + 1)),
            pl.BlockSpec((None, nblk, MOBA_HEADS, HEAD_DIM), lambda b: (b, 0, 0, 0)),
        ],
        out_specs=pl.BlockSpec((None, MOBA_HEADS, seq_len, LANES), lambda b: (b, 0, 0, 0)),
        out_shape=jax.ShapeDtypeStruct((seqs, MOBA_HEADS, seq_len, LANES), jnp.int32),
        compiler_params=pltpu.CompilerParams(dimension_semantics=("arbitrary",)),
        name="sample_select",
    )(proj, proj, kmean_l)


def _sample_attn_kernel(idx_ref, pt_ref, q_ref, kn_ref, vn_ref, ck_hbm, cv_hbm, o_ref, kbuf, vbuf, sem, *, layer, seq_len, n_pages):
    step = pl.program_id(0) * MOBA_HEADS + pl.program_id(1)
    n_steps = pl.num_programs(0) * MOBA_HEADS
    nsel = seq_len * MOBA_TOPK
    npg = nsel * PAGES_PER_BLOCK
    slot = step % 2

    def copies(s, slot_s, i):
        b, h = s // MOBA_HEADS, s % MOBA_HEADS
        blk = idx_ref[s * nsel + i // PAGES_PER_BLOCK]
        pg = pt_ref[b * n_pages + blk * PAGES_PER_BLOCK + i % PAGES_PER_BLOCK]
        return (pltpu.make_async_copy(ck_hbm.at[layer, pg, :, h, :], kbuf.at[slot_s, i], sem.at[0, slot_s]),
                pltpu.make_async_copy(cv_hbm.at[layer, pg, :, h, :], vbuf.at[slot_s, i], sem.at[1, slot_s]))

    def fetch(s, slot_s):
        for i in range(npg):
            ck, cv = copies(s, slot_s, i)
            ck.start()
            cv.start()

    @pl.when(step == 0)
    def _():
        fetch(step, slot)

    @pl.when(step + 1 < n_steps)
    def _():
        fetch(step + 1, 1 - slot)

    for i in range(npg):
        ck, cv = copies(step, slot, i)
        ck.wait()
        cv.wait()

    q = q_ref[...]
    scale = HEAD_DIM ** -0.5
    nkeys = npg * PAGE_SIZE
    per_q = MOBA_TOPK * MOBA_BLOCK
    kall = kbuf[slot].reshape(nkeys, HEAD_DIM)
    vall = vbuf[slot].reshape(nkeys, HEAD_DIM)
    logits = _dot_nt(q, kall) * scale
    row = lax.broadcasted_iota(jnp.int32, (seq_len, nkeys), 0)
    col = lax.broadcasted_iota(jnp.int32, (seq_len, nkeys), 1)
    allowed = (col >= row * per_q) & (col < (row + 1) * per_q)
    own = _dot_nt(q, kn_ref[...]) * scale
    orow = lax.broadcasted_iota(jnp.int32, (seq_len, seq_len), 0)
    ocol = lax.broadcasted_iota(jnp.int32, (seq_len, seq_len), 1)
    oallowed = ocol <= orow
    m = jnp.maximum(jnp.max(jnp.where(allowed, logits, MASK_VALUE), axis=1, keepdims=True),
                    jnp.max(jnp.where(oallowed, own, MASK_VALUE), axis=1, keepdims=True))
    p = jnp.where(allowed, jnp.exp(logits - m), 0.0)
    po = jnp.where(oallowed, jnp.exp(own - m), 0.0)
    l = jnp.sum(p, axis=1, keepdims=True) + jnp.sum(po, axis=1, keepdims=True)
    acc = jnp.dot(p, vall, preferred_element_type=F32) + jnp.dot(po, vn_ref[...], preferred_element_type=F32)
    o_ref[...] = (acc / l).astype(o_ref.dtype)


def _sample_attn(idx, page_table, proj, cache_k, cache_v, *, layer, seqs, seq_len):
    n = proj.shape[0]
    n_pages = page_table.shape[1]
    npg = seq_len * MOBA_TOPK * PAGES_PER_BLOCK
    hb = lambda off: off // HEAD_DIM
    grid_spec = pltpu.PrefetchScalarGridSpec(
        num_scalar_prefetch=2,
        grid=(seqs, MOBA_HEADS),
        in_specs=[
            pl.BlockSpec((seq_len, HEAD_DIM), lambda b, h, ix, pt: (b, hb(OFF_MQ) + h)),
            pl.BlockSpec((seq_len, HEAD_DIM), lambda b, h, ix, pt: (b, hb(OFF_MK) + h)),
            pl.BlockSpec((seq_len, HEAD_DIM), lambda b, h, ix, pt: (b, hb(OFF_MV) + h)),
            pl.BlockSpec(memory_space=pl.ANY),
            pl.BlockSpec(memory_space=pl.ANY),
        ],
        out_specs=pl.BlockSpec((seq_len, HEAD_DIM), lambda b, h, ix, pt: (b, h)),
        scratch_shapes=[pltpu.VMEM((2, npg, PAGE_SIZE, HEAD_DIM), F32), pltpu.VMEM((2, npg, PAGE_SIZE, HEAD_DIM), F32),
                        pltpu.SemaphoreType.DMA((2, 2))],
    )
    vm = _vmem_limit(10 * npg * _nbytes((PAGE_SIZE, HEAD_DIM), F32))
    return pl.pallas_call(
        functools.partial(_sample_attn_kernel, layer=layer, seq_len=seq_len, n_pages=n_pages),
        grid_spec=grid_spec,
        out_shape=jax.ShapeDtypeStruct((n, MOBA_WIDTH), F32),
        compiler_params=pltpu.CompilerParams(dimension_semantics=("arbitrary", "arbitrary"), vmem_limit_bytes=vm),
        name="sample_attn",
    )(idx, page_table.reshape(-1), proj, proj, proj, cache_k, cache_v)


def _outproj_kernel(og_ref, om_ref, oc_ref, w_ref, x_ref, gt_ref, g_ref, o_ref):
    acc = jnp.dot(og_ref[...].astype(BF16), w_ref[0:GLA_WIDTH, :], preferred_element_type=F32)
    acc = acc + jnp.dot(om_ref[...].astype(BF16), w_ref[GLA_WIDTH:GLA_WIDTH + MOBA_WIDTH, :], preferred_element_type=F32)
    acc = acc + jnp.dot(oc_ref[...].astype(BF16), w_ref[GLA_WIDTH + MOBA_WIDTH:D_MODEL, :], preferred_element_type=F32)
    o_ref[...] = x_ref[...] + gt_ref[...] * _rms(acc, g_ref[...])


def _outproj(og, om, oc, w, x, gt, g, *, tm, mod_spec):
    n = x.shape[0]
    mod1 = pl.BlockSpec(mod_spec.block_shape, lambda i: mod_spec.index_map(i, 0))
    vm = _vmem_limit(2 * _nbytes((D_MODEL, D_MODEL), BF16), 2 * _nbytes((tm, D_MODEL), BF16), 6 * _nbytes((tm, D_MODEL), F32))
    return pl.pallas_call(
        _outproj_kernel,
        grid=(n // tm,),
        in_specs=[
            pl.BlockSpec((tm, GLA_WIDTH), lambda i: (i, 0)),
            pl.BlockSpec((tm, MOBA_WIDTH), lambda i: (i, 0)),
            pl.BlockSpec((tm, CONV_CH), lambda i: (i, 0)),
            pl.BlockSpec((D_MODEL, D_MODEL), lambda i: (0, 0)),
            pl.BlockSpec((tm, D_MODEL), lambda i: (i, 0)),
            mod1,
            pl.BlockSpec((1, D_MODEL), lambda i: (0, 0)),
        ],
        out_specs=pl.BlockSpec((tm, D_MODEL), lambda i: (i, 0)),
        out_shape=jax.ShapeDtypeStruct((n, D_MODEL), F32),
        compiler_params=pltpu.CompilerParams(dimension_semantics=("arbitrary",), vmem_limit_bytes=vm),
        name="outproj",
    )(og, om, oc, w, x, gt, g)


FFN_TF = 512


def _ffn_kernel(x_ref, sh_ref, sc_ref, gt_ref, gpre_ref, gpost_ref, w1_ref, w3_ref, w2_ref, o_ref, h_scr, acc_scr):
    j = pl.program_id(1)

    @pl.when(j == 0)
    def _():
        h = _rms(x_ref[...], gpre_ref[...]) * (1.0 + sc_ref[...]) + sh_ref[...]
        h_scr[...] = h.astype(BF16)
        acc_scr[...] = jnp.zeros_like(acc_scr)

    h = h_scr[...]
    u = jnp.dot(h, w1_ref[...], preferred_element_type=F32)
    g = jnp.dot(h, w3_ref[...], preferred_element_type=F32)
    a = (_silu(u) * g).astype(BF16)
    acc_scr[...] += jnp.dot(a, w2_ref[...], preferred_element_type=F32)

    @pl.when(j == pl.num_programs(1) - 1)
    def _():
        o_ref[...] = x_ref[...] + gt_ref[...] * _rms(acc_scr[...], gpost_ref[...])


def _ffn(x, sh, sc, gt, gpre, gpost, w1, w3, w2, *, tm, mod_spec):
    n = x.shape[0]
    vec = pl.BlockSpec((1, D_MODEL), lambda i, j: (0, 0))
    vm = _vmem_limit(4 * _nbytes((tm, D_MODEL), F32), _nbytes((tm, D_MODEL), BF16), _nbytes((tm, D_MODEL), F32),
                     6 * _nbytes((D_MODEL, FFN_TF), BF16), 3 * _nbytes((tm, FFN_TF), F32))
    return pl.pallas_call(
        _ffn_kernel,
        grid=(n // tm, D_FF // FFN_TF),
        in_specs=[
            pl.BlockSpec((tm, D_MODEL), lambda i, j: (i, 0)),
            mod_spec, mod_spec, mod_spec, vec, vec,
            pl.BlockSpec((D_MODEL, FFN_TF), lambda i, j: (0, j)),
            pl.BlockSpec((D_MODEL, FFN_TF), lambda i, j: (0, j)),
            pl.BlockSpec((FFN_TF, D_MODEL), lambda i, j: (j, 0)),
        ],
        out_specs=pl.BlockSpec((tm, D_MODEL), lambda i, j: (i, 0)),
        out_shape=jax.ShapeDtypeStruct((n, D_MODEL), F32),
        scratch_shapes=[pltpu.VMEM((tm, D_MODEL), BF16), pltpu.VMEM((tm, D_MODEL), F32)],
        compiler_params=pltpu.CompilerParams(dimension_semantics=("arbitrary", "arbitrary"), vmem_limit_bytes=vm),
        name="ffn",
    )(x, sh, sc, gt, gpre, gpost, w1, w3, w2)


def _rope_tables(pos):
    inv = ROPE_THETA ** (-jnp.arange(0, HEAD_DIM, 2, dtype=F32) / HEAD_DIM)
    ang = pos.astype(F32)[:, None] * inv[None, :]
    cos, sin = jnp.cos(ang), jnp.sin(ang)
    return jnp.concatenate([cos, cos], axis=-1), jnp.concatenate([-sin, sin], axis=-1)


PROMPT_TM = 512
PROMPT_TB = 256


def kernel(x_prompt, x_sample, c_prompt, c_sample, cache_k, cache_v, state_gla, state_conv, page_table, w_ada, b_ada, g_pre_mix, g_post_mix, g_pre_ffn, g_post_ffn, w_in, w_a2, b_a, gla_norm_g, conv_w, conv_b, conv_ln_g, conv_ln_b, w_out, w_ffn1, w_ffn3, w_ffn2):
    bp, tp, _ = x_prompt.shape
    bs, ts, _ = x_sample.shape
    past = page_table.shape[1] * cache_k.shape[2]
    np_, ns = bp * tp, bs * ts
    assert tp % PROMPT_TM == 0 and tp % MOBA_BLOCK == 0 and past % MOBA_BLOCK == 0 and ts <= CONV_WIDTH - 1

    w_main = jnp.concatenate([w_in[:, :, :GATE_COL0], w_in[:, :, GATE_COL0 + GLA_GATE_RANK:]], axis=-1).astype(BF16)
    w_ga = jnp.pad(w_in[:, :, GATE_COL0:GATE_COL0 + GLA_GATE_RANK], ((0, 0), (0, 0), (0, LANES - GLA_GATE_RANK))).astype(BF16)
    wa2 = jnp.pad(w_a2, ((0, 0), (0, LANES - GLA_GATE_RANK), (0, 0)))
    w_out_b, w1_b, w3_b, w2_b = (w.astype(BF16) for w in (w_out, w_ffn1, w_ffn3, w_ffn2))

    rows = -(-(bp + bs) // SUBLANES) * SUBLANES
    c_all = jnp.pad(jnp.concatenate([c_prompt, c_sample], axis=0), ((0, rows - bp - bs), (0, 0)))
    mod = _modulation(c_all, w_ada, b_ada).reshape(DEPTH, rows, 6, D_MODEL)
    mod_p = mod[:, :bp].reshape(DEPTH, bp, 6, 1, D_MODEL)
    mod_s = jnp.repeat(mod[:, bp:bp + bs], ts, axis=1).transpose(0, 2, 1, 3)

    tiles_per_seq = tp // PROMPT_TM
    mod_spec_p = pl.BlockSpec((None, 1, D_MODEL), lambda i, j: (i // tiles_per_seq, 0, 0))
    mod_spec_s = pl.BlockSpec((ns, D_MODEL), lambda i, j: (0, 0))
    cos_p, sin_p = _rope_tables(jnp.arange(tp))
    cos_s, sin_s = _rope_tables(past + jnp.arange(ns) % ts)
    rope_spec_p = pl.BlockSpec((PROMPT_TM, HEAD_DIM), lambda i, j: (i % tiles_per_seq, 0))
    rope_spec_s = pl.BlockSpec((ns, HEAD_DIM), lambda i, j: (0, 0))

    kmean_past = _cache_kmean(page_table, cache_k)

    gla0 = jnp.zeros((bp, GLA_HEADS, GLA_DK, GLA_DV), F32)
    buf0 = jnp.zeros((bp, CONV_WIDTH - 1, CONV_CH), F32)

    def vec(a):
        return a.reshape(1, -1)

    xp = x_prompt.reshape(np_, D_MODEL)
    xs = x_sample.reshape(ns, D_MODEL)
    outs = {k: [] for k in ("sp", "cp", "ss", "cs")}
    kv_p = [jnp.zeros((DEPTH, np_, MOBA_HEADS, HEAD_DIM), F32) for _ in range(2)]
    kv_s = [jnp.zeros((DEPTH, ns, MOBA_HEADS, HEAD_DIM), F32) for _ in range(2)]
    for l in range(DEPTH):
        mp = [mod_p[l, :, k] for k in range(6)]
        proj, ga, *kv_p = _inproj(xp, mp[0], mp[1], vec(g_pre_mix[l]), w_main[l], w_ga[l], cos_p, sin_p, kv_p,
                                  layer=l, tm=PROMPT_TM, mod_spec=mod_spec_p, rope_spec=rope_spec_p)
        og, st = _gla(proj, ga, wa2[l], vec(b_a[l]), vec(gla_norm_g[l]), gla0, seqs=bp, seq_len=tp, tb=PROMPT_TB, act_dtype=BF16)
        oc, nb = _conv(proj, conv_w[l], vec(conv_b[l]), vec(conv_ln_g[l]), vec(conv_ln_b[l]), buf0, seqs=bp, seq_len=tp, tb=PROMPT_TB, act_dtype=BF16)
        om = _moba_prompt(proj, seqs=bp, seq_len=tp)
        xp = _outproj(og, om, oc, w_out_b[l], xp, mp[2], vec(g_post_mix[l]), tm=PROMPT_TM, mod_spec=mod_spec_p)
        xp = _ffn(xp, mp[3], mp[4], mp[5], vec(g_pre_ffn[l]), vec(g_post_ffn[l]), w1_b[l], w3_b[l], w2_b[l],
                  tm=PROMPT_TM, mod_spec=mod_spec_p)
        outs["sp"].append(st)
        outs["cp"].append(nb)

        ms = [mod_s[l, k] for k in range(6)]
        proj, ga, *kv_s = _inproj(xs, ms[0], ms[1], vec(g_pre_mix[l]), w_main[l], w_ga[l], cos_s, sin_s, kv_s,
                                  layer=l, tm=ns, mod_spec=mod_spec_s, rope_spec=rope_spec_s)
        og, st = _gla(proj, ga, wa2[l], vec(b_a[l]), vec(gla_norm_g[l]), state_gla[l], seqs=bs, seq_len=ts, tb=ts, act_dtype=F32)
        oc, nb = _conv(proj, conv_w[l], vec(conv_b[l]), vec(conv_ln_g[l]), vec(conv_ln_b[l]), state_conv[l], seqs=bs, seq_len=ts, tb=ts, act_dtype=F32)
        sel = _sample_select(proj, kmean_past[l], seqs=bs, seq_len=ts)
        idx = sel[:, :, :, :MOBA_TOPK].reshape(-1)
        om = _sample_attn(idx, page_table, proj, cache_k, cache_v, layer=l, seqs=bs, seq_len=ts)
        xs = _outproj(og, om, oc, w_out_b[l], xs, ms[2], vec(g_post_mix[l]), tm=ns, mod_spec=mod_spec_s)
        xs = _ffn(xs, ms[3], ms[4], ms[5], vec(g_pre_ffn[l]), vec(g_post_ffn[l]), w1_b[l], w3_b[l], w2_b[l],
                  tm=ns, mod_spec=mod_spec_s)
        outs["ss"].append(st)
        outs["cs"].append(nb)

    shape_p = (DEPTH, bp, tp, MOBA_HEADS, HEAD_DIM)
    shape_s = (DEPTH, bs, ts, MOBA_HEADS, HEAD_DIM)
    return (xp.reshape(bp, tp, D_MODEL), xs.reshape(bs, ts, D_MODEL),
            kv_p[0].reshape(shape_p), kv_p[1].reshape(shape_p),
            jnp.stack(outs["sp"]), jnp.stack(outs["cp"]),
            kv_s[0].reshape(shape_s), kv_s[1].reshape(shape_s),
            jnp.stack(outs["ss"]), jnp.stack(outs["cs"]))
```

```python
import functools
import math

import jax
import jax.numpy as jnp
from jax import lax
from jax.experimental import pallas as pl
from jax.experimental.pallas import tpu as pltpu

F32 = jnp.float32
BF16 = jnp.bfloat16
HIGHEST = lax.Precision.HIGHEST

D_MODEL = 2048
DEPTH = 4
GLA_HEADS = 4
GLA_DK = 64
GLA_DV = 128
GLA_QK = GLA_HEADS * GLA_DK
GLA_WIDTH = GLA_HEADS * GLA_DV
GLA_GATE_RANK = 16
GLA_TAU = 16.0
MOBA_HEADS = 8
HEAD_DIM = 128
MOBA_WIDTH = MOBA_HEADS * HEAD_DIM
MOBA_BLOCK = 256
MOBA_TOPK = 3
ROPE_THETA = 10000.0
CONV_CH = 512
CONV_WIDTH = 31
D_FF = 5632
RMS_EPS = 1e-6
LN_EPS = 1e-5
PAGE_SIZE = 128
PAGES_PER_BLOCK = MOBA_BLOCK // PAGE_SIZE

LANES = 128
SUBLANES = 8
VMEM_LIMIT_CAP = 60 * 1024 * 1024

PROJ_COLS = 2 * GLA_QK + 2 * GLA_WIDTH + 3 * MOBA_WIDTH + 2 * CONV_CH
OFF_GQ, OFF_GK, OFF_GV, OFF_GG = 0, 256, 512, 1024
OFF_MQ, OFF_MK, OFF_MV = 1536, 2560, 3584
OFF_CU, OFF_CG = 4608, 5120
GATE_COL0 = 2 * GLA_QK + 2 * GLA_WIDTH
PROJ_TN = 512
MASK_VALUE = -1e30

GLA_CHUNK = 16


def _vmem_limit(*nbytes):
    need = int(sum(nbytes) * 1.25) + (4 << 20)
    return min(max(need, 16 << 20), VMEM_LIMIT_CAP)


def _nbytes(shape, dtype):
    return math.prod(shape) * jnp.dtype(dtype).itemsize


def _sigmoid(x):
    return 1.0 / (1.0 + jnp.exp(-x))


def _silu(x):
    return x * _sigmoid(x)


def _rms(x, g):
    return x * lax.rsqrt(jnp.mean(x * x, axis=-1, keepdims=True) + RMS_EPS) * g


def _dot_nt(a, b, **kw):
    return lax.dot_general(a, b, (((1,), (1,)), ((), ())), preferred_element_type=F32, **kw)


def _dot_tn(a, b, **kw):
    return lax.dot_general(a, b, (((0,), (0,)), ((), ())), preferred_element_type=F32, **kw)


MOD_TN = 1024


def _mod_kernel(c_ref, w_ref, b_ref, o_ref):
    a = _silu(c_ref[...]).astype(BF16)
    o_ref[...] = jnp.dot(a, w_ref[...].astype(BF16), preferred_element_type=F32) + b_ref[...]


def _modulation(c_all, w_ada, b_ada):
    rows = c_all.shape[0]
    n = w_ada.shape[-1]
    return pl.pallas_call(
        _mod_kernel,
        grid=(DEPTH, n // MOD_TN),
        in_specs=[
            pl.BlockSpec((rows, D_MODEL), lambda l, j: (0, 0)),
            pl.BlockSpec((None, D_MODEL, MOD_TN), lambda l, j: (l, 0, j)),
            pl.BlockSpec((None, 1, MOD_TN), lambda l, j: (l, 0, j)),
        ],
        out_specs=pl.BlockSpec((None, rows, MOD_TN), lambda l, j: (l, 0, j)),
        out_shape=jax.ShapeDtypeStruct((DEPTH, rows, n), F32),
        compiler_params=pltpu.CompilerParams(
            dimension_semantics=("arbitrary", "arbitrary"),
            vmem_limit_bytes=_vmem_limit(2 * _nbytes((D_MODEL, MOD_TN), F32), _nbytes((D_MODEL, MOD_TN), BF16))),
        name="adaln_mod",
    )(c_all, w_ada, b_ada.reshape(DEPTH, 1, n))


HEADS_PER_TILE = PROJ_TN // HEAD_DIM


def _inproj_kernel(x_ref, sh_ref, sc_ref, g_ref, w_ref, wga_ref, cos_ref, sin_ref, kin_ref, vin_ref,
                   o_ref, ga_ref, ko_ref, vo_ref, h_scr):
    del kin_ref, vin_ref
    j = pl.program_id(1)

    @pl.when(j == 0)
    def _():
        h = _rms(x_ref[...], g_ref[...]) * (1.0 + sc_ref[...]) + sh_ref[...]
        hb = h.astype(BF16)
        h_scr[...] = hb
        ga_ref[...] = jnp.dot(hb, wga_ref[...], preferred_element_type=F32)

    acc = jnp.dot(h_scr[...], w_ref[...], preferred_element_type=F32)

    def slabs(rot):
        for s in range(HEADS_PER_TILE):
            a = acc[:, s * HEAD_DIM:(s + 1) * HEAD_DIM]
            if rot:
                a = a * cos_ref[...] + pltpu.roll(a, HEAD_DIM // 2, 1) * sin_ref[...]
            yield s, a

    for jj in range(PROJ_COLS // PROJ_TN):
        col = jj * PROJ_TN
        rot = OFF_MQ <= col < OFF_MV
        head_out = ko_ref if OFF_MK <= col < OFF_MV else vo_ref if OFF_MV <= col < OFF_CU else None
        head0 = (col - (OFF_MK if head_out is ko_ref else OFF_MV)) // HEAD_DIM

        @pl.when(j == jj)
        def _(rot=rot, head_out=head_out, head0=head0):
            if not rot and head_out is None:
                o_ref[...] = acc
                return
            for s, a in slabs(rot):
                o_ref[:, s * HEAD_DIM:(s + 1) * HEAD_DIM] = a
                if head_out is not None:
                    head_out[:, head0 + s, :] = a


def _inproj(x, sh, sc, g, w_main, w_ga, cos2, sin2, kv_bufs, *, layer, tm, mod_spec, rope_spec):
    n = x.shape[0]
    vm = _vmem_limit(2 * _nbytes((tm, D_MODEL), F32), _nbytes((tm, D_MODEL), BF16), 2 * _nbytes((D_MODEL, PROJ_TN), BF16),
                     2 * _nbytes((tm, PROJ_TN), F32), 2 * _nbytes((D_MODEL, LANES), BF16), 6 * _nbytes((tm, LANES), F32),
                     4 * _nbytes((tm, MOBA_WIDTH), F32),
                     4 * _nbytes(mod_spec.block_shape[-2:], F32) if mod_spec.block_shape[-2] != 1 else 0)
    kv_spec = pl.BlockSpec((None, tm, MOBA_HEADS, HEAD_DIM), lambda i, j: (layer, i, 0, 0))
    kv_shape = jax.ShapeDtypeStruct((DEPTH, n, MOBA_HEADS, HEAD_DIM), F32)
    n_in = 8
    return pl.pallas_call(
        _inproj_kernel,
        grid=(n // tm, PROJ_COLS // PROJ_TN),
        in_specs=[
            pl.BlockSpec((tm, D_MODEL), lambda i, j: (i, 0)),
            mod_spec, mod_spec,
            pl.BlockSpec((1, D_MODEL), lambda i, j: (0, 0)),
            pl.BlockSpec((None, D_MODEL, PROJ_TN), lambda i, j: (layer, 0, j)),
            pl.BlockSpec((None, D_MODEL, LANES), lambda i, j: (layer, 0, 0)),
            rope_spec, rope_spec,
            pl.BlockSpec(memory_space=pl.ANY), pl.BlockSpec(memory_space=pl.ANY),
        ],
        out_specs=[
            pl.BlockSpec((tm, PROJ_TN), lambda i, j: (i, j)),
            pl.BlockSpec((tm, LANES), lambda i, j: (i, 0)),
            kv_spec, kv_spec,
        ],
        out_shape=[jax.ShapeDtypeStruct((n, PROJ_COLS), F32), jax.ShapeDtypeStruct((n, LANES), F32), kv_shape, kv_shape],
        scratch_shapes=[pltpu.VMEM((tm, D_MODEL), BF16)],
        input_output_aliases={n_in: 2, n_in + 1: 3},
        compiler_params=pltpu.CompilerParams(dimension_semantics=("arbitrary", "arbitrary"), vmem_limit_bytes=vm),
        name="inproj",
    )(x, sh, sc, g, w_main, w_ga, cos2, sin2, *kv_bufs)


GLA_PAD = 16


def _gla_kernel(q_ref, k_ref, v_ref, gg_ref, ga_ref, wa2_ref, ba_ref, gn_ref, s0_ref, hm_ref, o_ref, st_ref,
                st_scr, b_scr, k_scr, v_scr, o_scr, *, tb, chunk):
    t = pl.program_id(1)
    pad = GLA_PAD

    @pl.when(t == 0)
    def _():
        for h in range(GLA_HEADS):
            st_scr[h] = s0_ref[h].T
        b_scr[0:pad, :] = jnp.zeros((pad, GLA_QK), F32)
        k_scr[0:pad, :] = jnp.zeros((pad, GLA_QK), F32)
        v_scr[0:pad, :] = jnp.zeros((pad, GLA_WIDTH), F32)

    xg = jnp.dot(ga_ref[...], wa2_ref[...], preferred_element_type=F32, precision=HIGHEST) + ba_ref[...]
    la = -(jnp.maximum(-xg, 0.0) + jnp.log1p(jnp.exp(-jnp.abs(xg)))) / GLA_TAU
    rowc = lax.broadcasted_iota(jnp.int32, (tb, GLA_QK), 0) & (chunk - 1)
    cur = pl.ds(pad, tb)

    b_scr[cur, :] = la
    step = 1
    while step < chunk:
        b_scr[cur, :] = b_scr[cur, :] + jnp.where(rowc >= step, b_scr[pl.ds(pad - step, tb), :], 0.0)
        step *= 2
    b = b_scr[cur, :]
    k = k_ref[...]
    k_scr[cur, :] = k
    v_scr[cur, :] = v_ref[...]
    qs = q_ref[...] * (GLA_DK ** -0.5)

    acc = jnp.zeros((tb, GLA_WIDTH), F32)
    for d in range(chunk):
        sh = pl.ds(pad - d, tb)
        e = jnp.exp(jnp.minimum(b - b_scr[sh, :], 0.0))
        term = jnp.where(rowc >= d, qs * k_scr[sh, :] * e, 0.0)
        acc = acc + jnp.dot(term.astype(BF16), hm_ref[...], preferred_element_type=F32) * v_scr[sh, :]
    o_scr[...] = acc

    eb = jnp.exp(b)
    states = [st_scr[h] for h in range(GLA_HEADS)]
    for n in range(tb // chunk):
        r0 = n * chunk
        bn = b[r0:r0 + chunk, :]
        bl = bn[chunk - 1:chunk, :]
        qd = qs[r0:r0 + chunk, :] * eb[r0:r0 + chunk, :]
        kd = k[r0:r0 + chunk, :] * jnp.exp(bl - bn)
        ebl = eb[r0 + chunk - 1:r0 + chunk, :]
        for h in range(GLA_HEADS):
            dsl = slice(h * GLA_DK, (h + 1) * GLA_DK)
            vsl = slice(h * GLA_DV, (h + 1) * GLA_DV)
            o_scr[r0:r0 + chunk, vsl] += _dot_nt(qd[:, dsl], states[h])
            states[h] = ebl[:, dsl] * states[h] + _dot_tn(v_ref[r0:r0 + chunk, vsl], kd[:, dsl])
    for h in range(GLA_HEADS):
        st_scr[h] = states[h]

    gn = gn_ref[...]
    for h in range(GLA_HEADS):
        vsl = slice(h * GLA_DV, (h + 1) * GLA_DV)
        o_ref[:, vsl] = (_rms(o_scr[:, vsl], gn) * _silu(gg_ref[:, vsl])).astype(o_ref.dtype)

    @pl.when(t == pl.num_programs(1) - 1)
    def _():
        for h in range(GLA_HEADS):
            st_ref[h] = states[h].T


def _gla_head_matrix():
    r = lax.broadcasted_iota(jnp.int32, (GLA_QK, GLA_WIDTH), 0) // GLA_DK
    c = lax.broadcasted_iota(jnp.int32, (GLA_QK, GLA_WIDTH), 1) // GLA_DV
    return (r == c).astype(BF16)


def _gla(proj, ga, wa2, ba, gn, s0, *, seqs, seq_len, tb, act_dtype):
    n = proj.shape[0]
    steps = seq_len // tb
    chunk = min(GLA_CHUNK, tb)
    assert chunk & (chunk - 1) == 0 and chunk - 1 <= GLA_PAD and tb % chunk == 0
    row = lambda b, t: b * steps + t
    kern = functools.partial(_gla_kernel, tb=tb, chunk=chunk)
    return pl.pallas_call(
        kern,
        grid=(seqs, steps),
        in_specs=[
            pl.BlockSpec((tb, GLA_QK), lambda b, t: (row(b, t), OFF_GQ // GLA_QK)),
            pl.BlockSpec((tb, GLA_QK), lambda b, t: (row(b, t), OFF_GK // GLA_QK)),
            pl.BlockSpec((tb, GLA_WIDTH), lambda b, t: (row(b, t), OFF_GV // GLA_WIDTH)),
            pl.BlockSpec((tb, GLA_WIDTH), lambda b, t: (row(b, t), OFF_GG // GLA_WIDTH)),
            pl.BlockSpec((tb, LANES), lambda b, t: (row(b, t), 0)),
            pl.BlockSpec((LANES, GLA_QK), lambda b, t: (0, 0)),
            pl.BlockSpec((1, GLA_QK), lambda b, t: (0, 0)),
            pl.BlockSpec((1, GLA_DV), lambda b, t: (0, 0)),
            pl.BlockSpec((None, GLA_HEADS, GLA_DK, GLA_DV), lambda b, t: (b, 0, 0, 0)),
            pl.BlockSpec((GLA_QK, GLA_WIDTH), lambda b, t: (0, 0)),
        ],
        out_specs=[
            pl.BlockSpec((tb, GLA_WIDTH), lambda b, t: (row(b, t), 0)),
            pl.BlockSpec((None, GLA_HEADS, GLA_DK, GLA_DV), lambda b, t: (b, 0, 0, 0)),
        ],
        out_shape=[jax.ShapeDtypeStruct((n, GLA_WIDTH), act_dtype),
                   jax.ShapeDtypeStruct((seqs, GLA_HEADS, GLA_DK, GLA_DV), F32)],
        scratch_shapes=[pltpu.VMEM((GLA_HEADS, GLA_DV, GLA_DK), F32),
                        pltpu.VMEM((GLA_PAD + tb, GLA_QK), F32), pltpu.VMEM((GLA_PAD + tb, GLA_QK), F32),
                        pltpu.VMEM((GLA_PAD + tb, GLA_WIDTH), F32), pltpu.VMEM((tb, GLA_WIDTH), F32)],
        compiler_params=pltpu.CompilerParams(dimension_semantics=("arbitrary", "arbitrary")),
        name="gla",
    )(proj, proj, proj, proj, ga, wa2, ba, gn, s0, _gla_head_matrix())


HALO = 32


def _conv_kernel(cu_ref, cg_ref, cw_ref, cb_ref, lg_ref, lb_ref, buf_ref, o_ref, nb_ref, ext, *, tb):
    t = pl.program_id(1)
    hist = CONV_WIDTH - 1

    @pl.when(t == 0)
    def _():
        ext[HALO - hist:HALO, :] = buf_ref[...]

    @pl.when(t > 0)
    def _():
        ext[0:HALO, :] = ext[tb:tb + HALO, :]

    ext[HALO:HALO + tb, :] = cu_ref[...] * _sigmoid(cg_ref[...])
    y = jnp.broadcast_to(cb_ref[...], (tb, CONV_CH))
    for w in range(CONV_WIDTH):
        y = y + ext[HALO - hist + w:HALO - hist + w + tb, :] * cw_ref[w:w + 1, :]
    nb_ref[...] = ext[HALO + tb - hist:HALO + tb, :]
    mu = jnp.mean(y, axis=-1, keepdims=True)
    yc = y - mu
    var = jnp.mean(yc * yc, axis=-1, keepdims=True)
    yn = yc * lax.rsqrt(var + LN_EPS) * lg_ref[...] + lb_ref[...]
    o_ref[...] = _silu(yn).astype(o_ref.dtype)


def _conv(proj, cw, cb, lg, lb, buf0, *, seqs, seq_len, tb, act_dtype):
    n = proj.shape[0]
    steps = seq_len // tb
    row = lambda b, t: b * steps + t
    hist = CONV_WIDTH - 1
    vec = pl.BlockSpec((1, CONV_CH), lambda b, t: (0, 0))
    return pl.pallas_call(
        functools.partial(_conv_kernel, tb=tb),
        grid=(seqs, steps),
        in_specs=[
            pl.BlockSpec((tb, CONV_CH), lambda b, t: (row(b, t), OFF_CU // CONV_CH)),
            pl.BlockSpec((tb, CONV_CH), lambda b, t: (row(b, t), OFF_CG // CONV_CH)),
            pl.BlockSpec((CONV_WIDTH, CONV_CH), lambda b, t: (0, 0)),
            vec, vec, vec,
            pl.BlockSpec((None, hist, CONV_CH), lambda b, t: (b, 0, 0)),
        ],
        out_specs=[
            pl.BlockSpec((tb, CONV_CH), lambda b, t: (row(b, t), 0)),
            pl.BlockSpec((None, hist, CONV_CH), lambda b, t: (b, 0, 0)),
        ],
        out_shape=[jax.ShapeDtypeStruct((n, CONV_CH), act_dtype), jax.ShapeDtypeStruct((seqs, hist, CONV_CH), F32)],
        scratch_shapes=[pltpu.VMEM((HALO + tb, CONV_CH), F32)],
        compiler_params=pltpu.CompilerParams(dimension_semantics=("arbitrary", "arbitrary")),
        name="conv",
    )(proj, proj, cw, cb, lg, lb, buf0)


def _top_blocks_t(s, n_past, blk_id):
    nblk = float(s.shape[0])
    past = blk_id < n_past
    s = jnp.where(past, s, -jnp.inf)
    sel = jnp.zeros(s.shape, F32)
    for _ in range(MOBA_TOPK):
        m = jnp.max(s, axis=0, keepdims=True)
        first = jnp.min(jnp.where(s == m, blk_id, nblk), axis=0, keepdims=True)
        pick = (blk_id == first) & past
        sel = jnp.where(pick, 1.0, sel)
        s = jnp.where(pick, -jnp.inf, s)
    return sel


MOBA_UNROLL = 4


KM_CHAINS = 8


def _past_block_means(page_refs, km_ref, step, n_steps, blocks_per_step):
    total_blocks = km_ref.shape[0] * km_ref.shape[1]
    for n in range(blocks_per_step):
        g = step * blocks_per_step + n

        def write(n=n, g=g):
            s = jnp.zeros((MOBA_HEADS, HEAD_DIM), F32)
            for p in range(PAGES_PER_BLOCK):
                page = page_refs[n * PAGES_PER_BLOCK + p][...]
                part = jnp.sum(page.reshape(KM_CHAINS, PAGE_SIZE // KM_CHAINS, MOBA_HEADS, HEAD_DIM), axis=1)
                s = s + jnp.sum(part, axis=0)
            per_seq = jnp.int32(km_ref.shape[1])
            km_ref[lax.div(g, per_seq), lax.rem(g, per_seq)] = s / MOBA_BLOCK

        if total_blocks == n_steps * blocks_per_step:
            write()
        else:
            pl.when(g < total_blocks)(write)


def _moba_prompt_kernel(pt_ref, q_ref, k_ref, v_ref, *rest, nblk, n_steps, blocks_per_step):
    del pt_ref
    n_pages = blocks_per_step * PAGES_PER_BLOCK
    page_refs = rest[:n_pages]
    o_ref, km_ref, kmean_scr, kb_scr, vt_scr, l_scr = rest[n_pages:]
    step = (pl.program_id(0) * pl.num_programs(1) + pl.program_id(1)) * pl.num_programs(2) + pl.program_id(2)
    _past_block_means(page_refs, km_ref, step, n_steps, blocks_per_step)
    _moba_prompt_body(q_ref, k_ref, v_ref, o_ref, kmean_scr, kb_scr, vt_scr, l_scr, nblk=nblk)


def _moba_prompt_body(q_ref, k_ref, v_ref, o_ref, kmean_scr, kb_scr, vt_scr, l_scr, *, nblk):
    qi = pl.program_id(2)
    blk = MOBA_BLOCK

    @pl.when(qi == 0)
    def _():
        for n in range(nblk):
            kblk = k_ref[n * blk:(n + 1) * blk, :]
            kmean_scr[n:n + 1, :] = jnp.mean(kblk, axis=0, keepdims=True)
            kb_scr[n] = kblk.astype(BF16)
            vt_scr[n] = v_ref[n * blk:(n + 1) * blk, :].T.astype(BF16)

    qt = q_ref[...].T
    blk_id = lax.broadcasted_iota(jnp.int32, (nblk, blk), 0)
    blk_f = blk_id.astype(F32)
    sel = _top_blocks_t(jnp.dot(kmean_scr[...], qt, preferred_element_type=F32, precision=HIGHEST), qi.astype(F32), blk_f)
    qtb = qt.astype(BF16)
    scale = HEAD_DIM ** -0.5
    key = lax.broadcasted_iota(jnp.int32, (blk, blk), 0)
    qry = lax.broadcasted_iota(jnp.int32, (blk, blk), 1)
    l_own = jnp.where(key <= qry, jnp.dot(kb_scr[qi], qtb, preferred_element_type=F32) * scale, MASK_VALUE)

    def fold(x, op):
        return op(x.reshape(blk // SUBLANES, SUBLANES, blk), axis=0)

    def pass1(kj, mmax):
        logits = jnp.dot(kb_scr[kj], qtb, preferred_element_type=F32) * scale
        picked = jnp.max(jnp.where(blk_id == kj, sel, 0.0), axis=0, keepdims=True)
        logits = jnp.where(picked > 0.5, logits, MASK_VALUE)
        l_scr[kj] = logits
        return jnp.maximum(mmax, fold(logits, jnp.max))

    def pass2(kj, carry):
        lsum, acc = carry
        p = jnp.exp(l_scr[kj] - m)
        return lsum + fold(p, jnp.sum), acc + jnp.dot(vt_scr[kj], p.astype(BF16), preferred_element_type=F32)

    def grouped(body, carry):
        def group(g, c):
            for u in range(MOBA_UNROLL):
                c = body(g * MOBA_UNROLL + u, c)
            return c
        return lax.fori_loop(0, (qi + MOBA_UNROLL - 1) // MOBA_UNROLL, group, carry)

    m = jnp.max(grouped(pass1, fold(l_own, jnp.max)), axis=0, keepdims=True)
    p_own = jnp.exp(l_own - m)
    lsum, acc = grouped(pass2, (fold(p_own, jnp.sum), jnp.dot(vt_scr[qi], p_own.astype(BF16), preferred_element_type=F32)))
    o_ref[...] = (acc / jnp.sum(lsum, axis=0, keepdims=True)).T.astype(o_ref.dtype)


def _moba_prompt(proj, page_table, cache_k, *, layer, seqs, seq_len):
    n = proj.shape[0]
    nblk = seq_len // MOBA_BLOCK
    assert nblk % MOBA_UNROLL == 0
    bsz, n_pages = page_table.shape
    blocks_per_seq = n_pages // PAGES_PER_BLOCK
    total_blocks = bsz * blocks_per_seq
    steps = seqs * MOBA_HEADS * nblk
    blocks_per_step = -(-total_blocks // steps)
    pages_per_step = blocks_per_step * PAGES_PER_BLOCK
    hb = lambda off: off // HEAD_DIM
    vm = _vmem_limit(4 * _nbytes((seq_len, HEAD_DIM), F32), 2 * _nbytes((seq_len, HEAD_DIM), BF16),
                     (nblk + 12) * _nbytes((MOBA_BLOCK, MOBA_BLOCK), F32),
                     2 * pages_per_step * _nbytes((PAGE_SIZE, MOBA_WIDTH), F32), 2 * total_blocks * _nbytes((MOBA_WIDTH,), F32))

    def page_spec(p):
        def index(b, h, i, pt):
            page = ((b * MOBA_HEADS + h) * nblk + i) * pages_per_step + p
            return (layer, pt[jnp.minimum(page, bsz * n_pages - 1)], 0, 0, 0)
        return pl.BlockSpec((None, None, PAGE_SIZE, MOBA_HEADS, HEAD_DIM), index)

    grid_spec = pltpu.PrefetchScalarGridSpec(
        num_scalar_prefetch=1,
        grid=(seqs, MOBA_HEADS, nblk),
        in_specs=[
            pl.BlockSpec((MOBA_BLOCK, HEAD_DIM), lambda b, h, i, pt: (b * nblk + i, hb(OFF_MQ) + h)),
            pl.BlockSpec((seq_len, HEAD_DIM), lambda b, h, i, pt: (b, hb(OFF_MK) + h)),
            pl.BlockSpec((seq_len, HEAD_DIM), lambda b, h, i, pt: (b, hb(OFF_MV) + h)),
        ] + [page_spec(p) for p in range(pages_per_step)],
        out_specs=[
            pl.BlockSpec((MOBA_BLOCK, HEAD_DIM), lambda b, h, i, pt: (b * nblk + i, h)),
            pl.BlockSpec((bsz, blocks_per_seq, MOBA_HEADS, HEAD_DIM), lambda b, h, i, pt: (0, 0, 0, 0)),
        ],
        scratch_shapes=[pltpu.VMEM((nblk, HEAD_DIM), F32), pltpu.VMEM((nblk, MOBA_BLOCK, HEAD_DIM), BF16),
                        pltpu.VMEM((nblk, HEAD_DIM, MOBA_BLOCK), BF16), pltpu.VMEM((nblk, MOBA_BLOCK, MOBA_BLOCK), F32)],
    )
    return pl.pallas_call(
        functools.partial(_moba_prompt_kernel, nblk=nblk, n_steps=steps, blocks_per_step=blocks_per_step),
        grid_spec=grid_spec,
        out_shape=[jax.ShapeDtypeStruct((n, MOBA_WIDTH), BF16),
                   jax.ShapeDtypeStruct((bsz, blocks_per_seq, MOBA_HEADS, HEAD_DIM), F32)],
        compiler_params=pltpu.CompilerParams(dimension_semantics=("arbitrary", "arbitrary", "arbitrary"), vmem_limit_bytes=vm),
        name="moba_prompt",
    )(page_table.reshape(-1), proj, proj, proj, *([cache_k] * pages_per_step))


def _select_kernel(qa_ref, qb_ref, km_ref, o_ref, *, nblk):
    rows = qa_ref.shape[0]
    half = MOBA_HEADS // 2
    lane = lax.broadcasted_iota(jnp.int32, (rows, nblk), 1).astype(F32)
    out_lane = lax.broadcasted_iota(jnp.int32, (rows, LANES), 1)
    for h in range(MOBA_HEADS):
        q_ref = qa_ref if h < half else qb_ref
        q = q_ref[:, (h % half) * HEAD_DIM:(h % half + 1) * HEAD_DIM]
        s = _dot_nt(q, km_ref[:, h, :], precision=HIGHEST)
        out = jnp.zeros((rows, LANES), F32)
        for j in range(MOBA_TOPK):
            m = jnp.max(s, axis=1, keepdims=True)
            first = jnp.min(jnp.where(s == m, lane, float(nblk)), axis=1, keepdims=True)
            out = jnp.where(out_lane == j, first, out)
            s = jnp.where(lane == first, -jnp.inf, s)
        o_ref[h] = out.astype(jnp.int32)


def _sample_select(proj, kmean_l, *, seqs, seq_len):
    nblk = kmean_l.shape[1]
    half_w = MOBA_WIDTH // 2
    return pl.pallas_call(
        functools.partial(_select_kernel, nblk=nblk),
        grid=(seqs,),
        in_specs=[
            pl.BlockSpec((seq_len, half_w), lambda b: (b, OFF_MQ // half_w)),
            pl.BlockSpec((seq_len, half_w), lambda b: (b, OFF_MQ // half_w + 1)),
            pl.BlockSpec((None, nblk, MOBA_HEADS, HEAD_DIM), lambda b: (b, 0, 0, 0)),
        ],
        out_specs=pl.BlockSpec((None, MOBA_HEADS, seq_len, LANES), lambda b: (b, 0, 0, 0)),
        out_shape=jax.ShapeDtypeStruct((seqs, MOBA_HEADS, seq_len, LANES), jnp.int32),
        compiler_params=pltpu.CompilerParams(dimension_semantics=("arbitrary",)),
        name="sample_select",
    )(proj, proj, kmean_l)


def _sample_attn_kernel(idx_ref, pt_ref, q_ref, kn_ref, vn_ref, ck_hbm, cv_hbm, o_ref, kbuf, vbuf, sem, *, layer, seq_len, n_pages):
    step = pl.program_id(0) * MOBA_HEADS + pl.program_id(1)
    n_steps = pl.num_programs(0) * MOBA_HEADS
    nsel = seq_len * MOBA_TOPK
    npg = nsel * PAGES_PER_BLOCK
    slot = step % 2

    def copies(s, slot_s, i):
        b, h = s // MOBA_HEADS, s % MOBA_HEADS
        blk = idx_ref[s * nsel + i // PAGES_PER_BLOCK]
        pg = pt_ref[b * n_pages + blk * PAGES_PER_BLOCK + i % PAGES_PER_BLOCK]
        return (pltpu.make_async_copy(ck_hbm.at[layer, pg, :, h, :], kbuf.at[slot_s, i], sem.at[0, slot_s]),
                pltpu.make_async_copy(cv_hbm.at[layer, pg, :, h, :], vbuf.at[slot_s, i], sem.at[1, slot_s]))

    def fetch(s, slot_s):
        for i in range(npg):
            ck, cv = copies(s, slot_s, i)
            ck.start()
            cv.start()

    @pl.when(step == 0)
    def _():
        fetch(step, slot)

    @pl.when(step + 1 < n_steps)
    def _():
        fetch(step + 1, 1 - slot)

    for i in range(npg):
        ck, cv = copies(step, slot, i)
        ck.wait()
        cv.wait()

    q = q_ref[...]
    scale = HEAD_DIM ** -0.5
    nkeys = npg * PAGE_SIZE
    per_q = MOBA_TOPK * MOBA_BLOCK
    kall = kbuf[slot].reshape(nkeys, HEAD_DIM)
    vall = vbuf[slot].reshape(nkeys, HEAD_DIM)
    logits = _dot_nt(q, kall) * scale
    row = lax.broadcasted_iota(jnp.int32, (seq_len, nkeys), 0)
    col = lax.broadcasted_iota(jnp.int32, (seq_len, nkeys), 1)
    allowed = (col >= row * per_q) & (col < (row + 1) * per_q)
    own = _dot_nt(q, kn_ref[...]) * scale
    orow = lax.broadcasted_iota(jnp.int32, (seq_len, seq_len), 0)
    ocol = lax.broadcasted_iota(jnp.int32, (seq_len, seq_len), 1)
    oallowed = ocol <= orow
    m = jnp.maximum(jnp.max(jnp.where(allowed, logits, MASK_VALUE), axis=1, keepdims=True),
                    jnp.max(jnp.where(oallowed, own, MASK_VALUE), axis=1, keepdims=True))
    p = jnp.where(allowed, jnp.exp(logits - m), 0.0)
    po = jnp.where(oallowed, jnp.exp(own - m), 0.0)
    l = jnp.sum(p, axis=1, keepdims=True) + jnp.sum(po, axis=1, keepdims=True)
    acc = jnp.dot(p, vall, preferred_element_type=F32) + jnp.dot(po, vn_ref[...], preferred_element_type=F32)
    o_ref[...] = (acc / l).astype(o_ref.dtype)


def _sample_attn(idx, page_table, proj, cache_k, cache_v, *, layer, seqs, seq_len):
    n = proj.shape[0]
    n_pages = page_table.shape[1]
    npg = seq_len * MOBA_TOPK * PAGES_PER_BLOCK
    hb = lambda off: off // HEAD_DIM
    grid_spec = pltpu.PrefetchScalarGridSpec(
        num_scalar_prefetch=2,
        grid=(seqs, MOBA_HEADS),
        in_specs=[
            pl.BlockSpec((seq_len, HEAD_DIM), lambda b, h, ix, pt: (b, hb(OFF_MQ) + h)),
            pl.BlockSpec((seq_len, HEAD_DIM), lambda b, h, ix, pt: (b, hb(OFF_MK) + h)),
            pl.BlockSpec((seq_len, HEAD_DIM), lambda b, h, ix, pt: (b, hb(OFF_MV) + h)),
            pl.BlockSpec(memory_space=pl.ANY),
            pl.BlockSpec(memory_space=pl.ANY),
        ],
        out_specs=pl.BlockSpec((seq_len, HEAD_DIM), lambda b, h, ix, pt: (b, h)),
        scratch_shapes=[pltpu.VMEM((2, npg, PAGE_SIZE, HEAD_DIM), F32), pltpu.VMEM((2, npg, PAGE_SIZE, HEAD_DIM), F32),
                        pltpu.SemaphoreType.DMA((2, 2))],
    )
    vm = _vmem_limit(10 * npg * _nbytes((PAGE_SIZE, HEAD_DIM), F32))
    return pl.pallas_call(
        functools.partial(_sample_attn_kernel, layer=layer, seq_len=seq_len, n_pages=n_pages),
        grid_spec=grid_spec,
        out_shape=jax.ShapeDtypeStruct((n, MOBA_WIDTH), F32),
        compiler_params=pltpu.CompilerParams(dimension_semantics=("arbitrary", "arbitrary"), vmem_limit_bytes=vm),
        name="sample_attn",
    )(idx, page_table.reshape(-1), proj, proj, proj, cache_k, cache_v)


def _outproj_kernel(og_ref, om_ref, oc_ref, w_ref, x_ref, gt_ref, g_ref, o_ref):
    acc = jnp.dot(og_ref[...].astype(BF16), w_ref[0:GLA_WIDTH, :], preferred_element_type=F32)
    acc = acc + jnp.dot(om_ref[...].astype(BF16), w_ref[GLA_WIDTH:GLA_WIDTH + MOBA_WIDTH, :], preferred_element_type=F32)
    acc = acc + jnp.dot(oc_ref[...].astype(BF16), w_ref[GLA_WIDTH + MOBA_WIDTH:D_MODEL, :], preferred_element_type=F32)
    o_ref[...] = x_ref[...] + gt_ref[...] * _rms(acc, g_ref[...])


def _outproj(og, om, oc, w, x, gt, g, *, layer, tm, mod_spec):
    n = x.shape[0]
    mod1 = pl.BlockSpec(mod_spec.block_shape, lambda i: mod_spec.index_map(i, 0))
    vm = _vmem_limit(2 * _nbytes((D_MODEL, D_MODEL), BF16), 2 * _nbytes((tm, D_MODEL), BF16), 6 * _nbytes((tm, D_MODEL), F32))
    return pl.pallas_call(
        _outproj_kernel,
        grid=(n // tm,),
        in_specs=[
            pl.BlockSpec((tm, GLA_WIDTH), lambda i: (i, 0)),
            pl.BlockSpec((tm, MOBA_WIDTH), lambda i: (i, 0)),
            pl.BlockSpec((tm, CONV_CH), lambda i: (i, 0)),
            pl.BlockSpec((None, D_MODEL, D_MODEL), lambda i: (layer, 0, 0)),
            pl.BlockSpec((tm, D_MODEL), lambda i: (i, 0)),
            mod1,
            pl.BlockSpec((1, D_MODEL), lambda i: (0, 0)),
        ],
        out_specs=pl.BlockSpec((tm, D_MODEL), lambda i: (i, 0)),
        out_shape=jax.ShapeDtypeStruct((n, D_MODEL), F32),
        compiler_params=pltpu.CompilerParams(dimension_semantics=("arbitrary",), vmem_limit_bytes=vm),
        name="outproj",
    )(og, om, oc, w, x, gt, g)


FFN_TF = 512


def _ffn_kernel(x_ref, sh_ref, sc_ref, gt_ref, gpre_ref, gpost_ref, w1_ref, w3_ref, w2_ref, o_ref, h_scr, acc_scr):
    j = pl.program_id(1)

    @pl.when(j == 0)
    def _():
        h = _rms(x_ref[...], gpre_ref[...]) * (1.0 + sc_ref[...]) + sh_ref[...]
        h_scr[...] = h.astype(BF16)
        acc_scr[...] = jnp.zeros_like(acc_scr)

    h = h_scr[...]
    u = jnp.dot(h, w1_ref[...], preferred_element_type=F32)
    g = jnp.dot(h, w3_ref[...], preferred_element_type=F32)
    a = (_silu(u) * g).astype(BF16)
    acc_scr[...] += jnp.dot(a, w2_ref[...], preferred_element_type=F32)

    @pl.when(j == pl.num_programs(1) - 1)
    def _():
        o_ref[...] = x_ref[...] + gt_ref[...] * _rms(acc_scr[...], gpost_ref[...])


def _ffn(x, sh, sc, gt, gpre, gpost, w1, w3, w2, *, layer, tm, mod_spec):
    n = x.shape[0]
    vec = pl.BlockSpec((1, D_MODEL), lambda i, j: (0, 0))
    vm = _vmem_limit(4 * _nbytes((tm, D_MODEL), F32), _nbytes((tm, D_MODEL), BF16), _nbytes((tm, D_MODEL), F32),
                     6 * _nbytes((D_MODEL, FFN_TF), BF16), 3 * _nbytes((tm, FFN_TF), F32))
    return pl.pallas_call(
        _ffn_kernel,
        grid=(n // tm, D_FF // FFN_TF),
        in_specs=[
            pl.BlockSpec((tm, D_MODEL), lambda i, j: (i, 0)),
            mod_spec, mod_spec, mod_spec, vec, vec,
            pl.BlockSpec((None, D_MODEL, FFN_TF), lambda i, j: (layer, 0, j)),
            pl.BlockSpec((None, D_MODEL, FFN_TF), lambda i, j: (layer, 0, j)),
            pl.BlockSpec((None, FFN_TF, D_MODEL), lambda i, j: (layer, j, 0)),
        ],
        out_specs=pl.BlockSpec((tm, D_MODEL), lambda i, j: (i, 0)),
        out_shape=jax.ShapeDtypeStruct((n, D_MODEL), F32),
        scratch_shapes=[pltpu.VMEM((tm, D_MODEL), BF16), pltpu.VMEM((tm, D_MODEL), F32)],
        compiler_params=pltpu.CompilerParams(dimension_semantics=("arbitrary", "arbitrary"), vmem_limit_bytes=vm),
        name="ffn",
    )(x, sh, sc, gt, gpre, gpost, w1, w3, w2)


def _rope_tables(pos):
    inv = ROPE_THETA ** (-jnp.arange(0, HEAD_DIM, 2, dtype=F32) / HEAD_DIM)
    ang = pos.astype(F32)[:, None] * inv[None, :]
    cos, sin = jnp.cos(ang), jnp.sin(ang)
    return jnp.concatenate([cos, cos], axis=-1), jnp.concatenate([-sin, sin], axis=-1)


PROMPT_TM = 512
PROMPT_TM_IN = 1024
PROMPT_TB = 256


def kernel(x_prompt, x_sample, c_prompt, c_sample, cache_k, cache_v, state_gla, state_conv, page_table, w_ada, b_ada, g_pre_mix, g_post_mix, g_pre_ffn, g_post_ffn, w_in, w_a2, b_a, gla_norm_g, conv_w, conv_b, conv_ln_g, conv_ln_b, w_out, w_ffn1, w_ffn3, w_ffn2):
    bp, tp, _ = x_prompt.shape
    bs, ts, _ = x_sample.shape
    past = page_table.shape[1] * cache_k.shape[2]
    np_, ns = bp * tp, bs * ts
    assert tp % PROMPT_TM == 0 and tp % PROMPT_TM_IN == 0 and tp % MOBA_BLOCK == 0 and past % MOBA_BLOCK == 0 and ts <= CONV_WIDTH - 1

    w_main = jnp.concatenate([w_in[:, :, :GATE_COL0], w_in[:, :, GATE_COL0 + GLA_GATE_RANK:]], axis=-1).astype(BF16)
    w_ga = jnp.pad(w_in[:, :, GATE_COL0:GATE_COL0 + GLA_GATE_RANK], ((0, 0), (0, 0), (0, LANES - GLA_GATE_RANK))).astype(BF16)
    wa2 = jnp.pad(w_a2, ((0, 0), (0, LANES - GLA_GATE_RANK), (0, 0)))
    w_out_b, w1_b, w3_b, w2_b = (w.astype(BF16) for w in (w_out, w_ffn1, w_ffn3, w_ffn2))

    rows = -(-(bp + bs) // SUBLANES) * SUBLANES
    c_all = jnp.pad(jnp.concatenate([c_prompt, c_sample], axis=0), ((0, rows - bp - bs), (0, 0)))
    mod = _modulation(c_all, w_ada, b_ada).reshape(DEPTH, rows, 6, D_MODEL)
    mod_p = mod[:, :bp].reshape(DEPTH, bp, 6, 1, D_MODEL)
    mod_s = jnp.repeat(mod[:, bp:bp + bs], ts, axis=1).transpose(0, 2, 1, 3)

    def mod_spec_p(tm):
        return pl.BlockSpec((None, 1, D_MODEL), lambda i, j: (i // (tp // tm), 0, 0))

    mod_spec_s = pl.BlockSpec((ns, D_MODEL), lambda i, j: (0, 0))
    cos_p, sin_p = _rope_tables(jnp.arange(tp))
    cos_s, sin_s = _rope_tables(past + jnp.arange(ns) % ts)
    rope_spec_p = pl.BlockSpec((PROMPT_TM_IN, HEAD_DIM), lambda i, j: (i % (tp // PROMPT_TM_IN), 0))
    rope_spec_s = pl.BlockSpec((ns, HEAD_DIM), lambda i, j: (0, 0))

    gla0 = jnp.zeros((bp, GLA_HEADS, GLA_DK, GLA_DV), F32)
    buf0 = jnp.zeros((bp, CONV_WIDTH - 1, CONV_CH), F32)

    def vec(a):
        return a.reshape(1, -1)

    xp = x_prompt.reshape(np_, D_MODEL)
    xs = x_sample.reshape(ns, D_MODEL)
    outs = {k: [] for k in ("sp", "cp", "ss", "cs")}
    kv_p = [jnp.zeros((DEPTH, np_, MOBA_HEADS, HEAD_DIM), F32) for _ in range(2)]
    kv_s = [jnp.zeros((DEPTH, ns, MOBA_HEADS, HEAD_DIM), F32) for _ in range(2)]
    for l in range(DEPTH):
        mp = [mod_p[l, :, k] for k in range(6)]
        proj, ga, *kv_p = _inproj(xp, mp[0], mp[1], vec(g_pre_mix[l]), w_main, w_ga, cos_p, sin_p, kv_p,
                                  layer=l, tm=PROMPT_TM_IN, mod_spec=mod_spec_p(PROMPT_TM_IN), rope_spec=rope_spec_p)
        og, st = _gla(proj, ga, wa2[l], vec(b_a[l]), vec(gla_norm_g[l]), gla0, seqs=bp, seq_len=tp, tb=PROMPT_TB, act_dtype=BF16)
        oc, nb = _conv(proj, conv_w[l], vec(conv_b[l]), vec(conv_ln_g[l]), vec(conv_ln_b[l]), buf0, seqs=bp, seq_len=tp, tb=PROMPT_TB, act_dtype=BF16)
        om, kmean_past = _moba_prompt(proj, page_table, cache_k, layer=l, seqs=bp, seq_len=tp)
        xp = _outproj(og, om, oc, w_out_b, xp, mp[2], vec(g_post_mix[l]), layer=l, tm=PROMPT_TM, mod_spec=mod_spec_p(PROMPT_TM))
        xp = _ffn(xp, mp[3], mp[4], mp[5], vec(g_pre_ffn[l]), vec(g_post_ffn[l]), w1_b, w3_b, w2_b,
                  layer=l, tm=PROMPT_TM, mod_spec=mod_spec_p(PROMPT_TM))
        outs["sp"].append(st)
        outs["cp"].append(nb)

        ms = [mod_s[l, k] for k in range(6)]
        proj, ga, *kv_s = _inproj(xs, ms[0], ms[1], vec(g_pre_mix[l]), w_main, w_ga, cos_s, sin_s, kv_s,
                                  layer=l, tm=ns, mod_spec=mod_spec_s, rope_spec=rope_spec_s)
        og, st = _gla(proj, ga, wa2[l], vec(b_a[l]), vec(gla_norm_g[l]), state_gla[l], seqs=bs, seq_len=ts, tb=ts, act_dtype=F32)
        oc, nb = _conv(proj, conv_w[l], vec(conv_b[l]), vec(conv_ln_g[l]), vec(conv_ln_b[l]), state_conv[l], seqs=bs, seq_len=ts, tb=ts, act_dtype=F32)
        sel = _sample_select(proj, kmean_past, seqs=bs, seq_len=ts)
        idx = sel[:, :, :, :MOBA_TOPK].reshape(-1)
        om = _sample_attn(idx, page_table, proj, cache_k, cache_v, layer=l, seqs=bs, seq_len=ts)
        xs = _outproj(og, om, oc, w_out_b, xs, ms[2], vec(g_post_mix[l]), layer=l, tm=ns, mod_spec=mod_spec_s)
        xs = _ffn(xs, ms[3], ms[4], ms[5], vec(g_pre_ffn[l]), vec(g_post_ffn[l]), w1_b, w3_b, w2_b,
                  layer=l, tm=ns, mod_spec=mod_spec_s)
        outs["ss"].append(st)
        outs["cs"].append(nb)

    shape_p = (DEPTH, bp, tp, MOBA_HEADS, HEAD_DIM)
    shape_s = (DEPTH, bs, ts, MOBA_HEADS, HEAD_DIM)
    return (xp.reshape(bp, tp, D_MODEL), xs.reshape(bs, ts, D_MODEL),
            kv_p[0].reshape(shape_p), kv_p[1].reshape(shape_p),
            jnp.stack(outs["sp"]), jnp.stack(outs["cp"]),
            kv_s[0].reshape(shape_s), kv_s[1].reshape(shape_s),
            jnp.stack(outs["ss"]), jnp.stack(outs["cs"]))
```

```python
import functools
import math

import jax
import jax.numpy as jnp
from jax import lax
from jax.experimental import pallas as pl
from jax.experimental.pallas import tpu as pltpu

F32 = jnp.float32
BF16 = jnp.bfloat16
HIGHEST = lax.Precision.HIGHEST

D_MODEL = 2048
DEPTH = 4
GLA_HEADS = 4
GLA_DK = 64
GLA_DV = 128
GLA_QK = GLA_HEADS * GLA_DK
GLA_WIDTH = GLA_HEADS * GLA_DV
GLA_GATE_RANK = 16
GLA_TAU = 16.0
MOBA_HEADS = 8
HEAD_DIM = 128
MOBA_WIDTH = MOBA_HEADS * HEAD_DIM
MOBA_BLOCK = 256
MOBA_TOPK = 3
ROPE_THETA = 10000.0
CONV_CH = 512
CONV_WIDTH = 31
D_FF = 5632
RMS_EPS = 1e-6
LN_EPS = 1e-5
PAGE_SIZE = 128
PAGES_PER_BLOCK = MOBA_BLOCK // PAGE_SIZE

LANES = 128
SUBLANES = 8
VMEM_LIMIT_CAP = 60 * 1024 * 1024

PROJ_COLS = 2 * GLA_QK + 2 * GLA_WIDTH + 3 * MOBA_WIDTH + 2 * CONV_CH
OFF_GQ, OFF_GK, OFF_GV, OFF_GG = 0, 256, 512, 1024
OFF_MQ, OFF_MK, OFF_MV = 1536, 2560, 3584
OFF_CU, OFF_CG = 4608, 5120
GATE_COL0 = 2 * GLA_QK + 2 * GLA_WIDTH
PROJ_TN = 512
MASK_VALUE = -1e30

GLA_CHUNK = 16


def _vmem_limit(*nbytes):
    need = int(sum(nbytes) * 1.25) + (4 << 20)
    return min(max(need, 16 << 20), VMEM_LIMIT_CAP)


def _nbytes(shape, dtype):
    return math.prod(shape) * jnp.dtype(dtype).itemsize


def _sigmoid(x):
    return 1.0 / (1.0 + jnp.exp(-x))


def _silu(x):
    return x * _sigmoid(x)


def _rms(x, g):
    return x * lax.rsqrt(jnp.mean(x * x, axis=-1, keepdims=True) + RMS_EPS) * g


def _dot_nt(a, b, **kw):
    return lax.dot_general(a, b, (((1,), (1,)), ((), ())), preferred_element_type=F32, **kw)


def _dot_tn(a, b, **kw):
    return lax.dot_general(a, b, (((0,), (0,)), ((), ())), preferred_element_type=F32, **kw)


MOD_TN = 1024


def _mod_kernel(c_ref, w_ref, b_ref, o_ref):
    a = _silu(c_ref[...]).astype(BF16)
    o_ref[...] = jnp.dot(a, w_ref[...].astype(BF16), preferred_element_type=F32) + b_ref[...]


def _modulation(c_all, w_ada, b_ada):
    rows = c_all.shape[0]
    n = w_ada.shape[-1]
    return pl.pallas_call(
        _mod_kernel,
        grid=(DEPTH, n // MOD_TN),
        in_specs=[
            pl.BlockSpec((rows, D_MODEL), lambda l, j: (0, 0)),
            pl.BlockSpec((None, D_MODEL, MOD_TN), lambda l, j: (l, 0, j)),
            pl.BlockSpec((None, 1, MOD_TN), lambda l, j: (l, 0, j)),
        ],
        out_specs=pl.BlockSpec((None, rows, MOD_TN), lambda l, j: (l, 0, j)),
        out_shape=jax.ShapeDtypeStruct((DEPTH, rows, n), F32),
        compiler_params=pltpu.CompilerParams(
            dimension_semantics=("arbitrary", "arbitrary"),
            vmem_limit_bytes=_vmem_limit(2 * _nbytes((D_MODEL, MOD_TN), F32), _nbytes((D_MODEL, MOD_TN), BF16))),
        name="adaln_mod",
    )(c_all, w_ada, b_ada.reshape(DEPTH, 1, n))


HEADS_PER_TILE = PROJ_TN // HEAD_DIM


def _inproj_kernel(x_ref, sh_ref, sc_ref, g_ref, w_ref, wga_ref, cos_ref, sin_ref, kin_ref, vin_ref,
                   o_ref, ga_ref, ko_ref, vo_ref, h_scr):
    del kin_ref, vin_ref
    j = pl.program_id(1)

    @pl.when(j == 0)
    def _():
        h = _rms(x_ref[...], g_ref[...]) * (1.0 + sc_ref[...]) + sh_ref[...]
        hb = h.astype(BF16)
        h_scr[...] = hb
        ga_ref[...] = jnp.dot(hb, wga_ref[...], preferred_element_type=F32)

    acc = jnp.dot(h_scr[...], w_ref[...], preferred_element_type=F32)

    def slabs(rot):
        for s in range(HEADS_PER_TILE):
            a = acc[:, s * HEAD_DIM:(s + 1) * HEAD_DIM]
            if rot:
                a = a * cos_ref[...] + pltpu.roll(a, HEAD_DIM // 2, 1) * sin_ref[...]
            yield s, a

    for jj in range(PROJ_COLS // PROJ_TN):
        col = jj * PROJ_TN
        rot = OFF_MQ <= col < OFF_MV
        head_out = ko_ref if OFF_MK <= col < OFF_MV else vo_ref if OFF_MV <= col < OFF_CU else None
        head0 = (col - (OFF_MK if head_out is ko_ref else OFF_MV)) // HEAD_DIM

        @pl.when(j == jj)
        def _(rot=rot, head_out=head_out, head0=head0):
            if not rot and head_out is None:
                o_ref[...] = acc
                return
            for s, a in slabs(rot):
                o_ref[:, s * HEAD_DIM:(s + 1) * HEAD_DIM] = a
                if head_out is not None:
                    head_out[:, head0 + s, :] = a


def _inproj(x, sh, sc, g, w_main, w_ga, cos2, sin2, kv_bufs, *, layer, tm, mod_spec, rope_spec):
    n = x.shape[0]
    vm = _vmem_limit(2 * _nbytes((tm, D_MODEL), F32), _nbytes((tm, D_MODEL), BF16), 2 * _nbytes((D_MODEL, PROJ_TN), BF16),
                     2 * _nbytes((tm, PROJ_TN), F32), 2 * _nbytes((D_MODEL, LANES), BF16), 6 * _nbytes((tm, LANES), F32),
                     4 * _nbytes((tm, MOBA_WIDTH), F32),
                     4 * _nbytes(mod_spec.block_shape[-2:], F32) if mod_spec.block_shape[-2] != 1 else 0)
    kv_spec = pl.BlockSpec((None, tm, MOBA_HEADS, HEAD_DIM), lambda i, j: (layer, i, 0, 0))
    kv_shape = jax.ShapeDtypeStruct((DEPTH, n, MOBA_HEADS, HEAD_DIM), F32)
    n_in = 8
    return pl.pallas_call(
        _inproj_kernel,
        grid=(n // tm, PROJ_COLS // PROJ_TN),
        in_specs=[
            pl.BlockSpec((tm, D_MODEL), lambda i, j: (i, 0)),
            mod_spec, mod_spec,
            pl.BlockSpec((1, D_MODEL), lambda i, j: (0, 0)),
            pl.BlockSpec((None, D_MODEL, PROJ_TN), lambda i, j: (layer, 0, j)),
            pl.BlockSpec((None, D_MODEL, LANES), lambda i, j: (layer, 0, 0)),
            rope_spec, rope_spec,
            pl.BlockSpec(memory_space=pl.ANY), pl.BlockSpec(memory_space=pl.ANY),
        ],
        out_specs=[
            pl.BlockSpec((tm, PROJ_TN), lambda i, j: (i, j)),
            pl.BlockSpec((tm, LANES), lambda i, j: (i, 0)),
            kv_spec, kv_spec,
        ],
        out_shape=[jax.ShapeDtypeStruct((n, PROJ_COLS), F32), jax.ShapeDtypeStruct((n, LANES), F32), kv_shape, kv_shape],
        scratch_shapes=[pltpu.VMEM((tm, D_MODEL), BF16)],
        input_output_aliases={n_in: 2, n_in + 1: 3},
        compiler_params=pltpu.CompilerParams(dimension_semantics=("arbitrary", "arbitrary"), vmem_limit_bytes=vm),
        name="inproj",
    )(x, sh, sc, g, w_main, w_ga, cos2, sin2, *kv_bufs)


GLA_PAD = 16


def _gla_kernel(q_ref, k_ref, v_ref, gg_ref, ga_ref, wa2_ref, ba_ref, gn_ref, s0_ref, hm_ref, o_ref, st_ref,
                st_scr, b_scr, k_scr, v_scr, o_scr, *, tb, chunk):
    t = pl.program_id(1)
    pad = GLA_PAD

    @pl.when(t == 0)
    def _():
        for h in range(GLA_HEADS):
            st_scr[h] = s0_ref[h].T
        b_scr[0:pad, :] = jnp.zeros((pad, GLA_QK), F32)
        k_scr[0:pad, :] = jnp.zeros((pad, GLA_QK), F32)
        v_scr[0:pad, :] = jnp.zeros((pad, GLA_WIDTH), F32)

    xg = jnp.dot(ga_ref[...], wa2_ref[...], preferred_element_type=F32, precision=HIGHEST) + ba_ref[...]
    la = -(jnp.maximum(-xg, 0.0) + jnp.log1p(jnp.exp(-jnp.abs(xg)))) / GLA_TAU
    rowc = lax.broadcasted_iota(jnp.int32, (tb, GLA_QK), 0) & (chunk - 1)
    cur = pl.ds(pad, tb)

    b_scr[cur, :] = la
    step = 1
    while step < chunk:
        b_scr[cur, :] = b_scr[cur, :] + jnp.where(rowc >= step, b_scr[pl.ds(pad - step, tb), :], 0.0)
        step *= 2
    b = b_scr[cur, :]
    k = k_ref[...]
    k_scr[cur, :] = k
    v_scr[cur, :] = v_ref[...]
    qs = q_ref[...] * (GLA_DK ** -0.5)

    acc = jnp.zeros((tb, GLA_WIDTH), F32)
    for d in range(chunk):
        sh = pl.ds(pad - d, tb)
        e = jnp.exp(jnp.minimum(b - b_scr[sh, :], 0.0))
        term = jnp.where(rowc >= d, qs * k_scr[sh, :] * e, 0.0)
        acc = acc + jnp.dot(term.astype(BF16), hm_ref[...], preferred_element_type=F32) * v_scr[sh, :]
    o_scr[...] = acc

    eb = jnp.exp(b)
    states = [st_scr[h] for h in range(GLA_HEADS)]
    for n in range(tb // chunk):
        r0 = n * chunk
        bn = b[r0:r0 + chunk, :]
        bl = bn[chunk - 1:chunk, :]
        qd = qs[r0:r0 + chunk, :] * eb[r0:r0 + chunk, :]
        kd = k[r0:r0 + chunk, :] * jnp.exp(bl - bn)
        ebl = eb[r0 + chunk - 1:r0 + chunk, :]
        for h in range(GLA_HEADS):
            dsl = slice(h * GLA_DK, (h + 1) * GLA_DK)
            vsl = slice(h * GLA_DV, (h + 1) * GLA_DV)
            o_scr[r0:r0 + chunk, vsl] += _dot_nt(qd[:, dsl], states[h])
            states[h] = ebl[:, dsl] * states[h] + _dot_tn(v_ref[r0:r0 + chunk, vsl], kd[:, dsl])
    for h in range(GLA_HEADS):
        st_scr[h] = states[h]

    gn = gn_ref[...]
    for h in range(GLA_HEADS):
        vsl = slice(h * GLA_DV, (h + 1) * GLA_DV)
        o_ref[:, vsl] = (_rms(o_scr[:, vsl], gn) * _silu(gg_ref[:, vsl])).astype(o_ref.dtype)

    @pl.when(t == pl.num_programs(1) - 1)
    def _():
        for h in range(GLA_HEADS):
            st_ref[h] = states[h].T


def _gla_head_matrix():
    r = lax.broadcasted_iota(jnp.int32, (GLA_QK, GLA_WIDTH), 0) // GLA_DK
    c = lax.broadcasted_iota(jnp.int32, (GLA_QK, GLA_WIDTH), 1) // GLA_DV
    return (r == c).astype(BF16)


def _gla(proj, ga, wa2, ba, gn, s0, *, seqs, seq_len, tb, act_dtype):
    n = proj.shape[0]
    steps = seq_len // tb
    chunk = min(GLA_CHUNK, tb)
    assert chunk & (chunk - 1) == 0 and chunk - 1 <= GLA_PAD and tb % chunk == 0
    row = lambda b, t: b * steps + t
    kern = functools.partial(_gla_kernel, tb=tb, chunk=chunk)
    return pl.pallas_call(
        kern,
        grid=(seqs, steps),
        in_specs=[
            pl.BlockSpec((tb, GLA_QK), lambda b, t: (row(b, t), OFF_GQ // GLA_QK)),
            pl.BlockSpec((tb, GLA_QK), lambda b, t: (row(b, t), OFF_GK // GLA_QK)),
            pl.BlockSpec((tb, GLA_WIDTH), lambda b, t: (row(b, t), OFF_GV // GLA_WIDTH)),
            pl.BlockSpec((tb, GLA_WIDTH), lambda b, t: (row(b, t), OFF_GG // GLA_WIDTH)),
            pl.BlockSpec((tb, LANES), lambda b, t: (row(b, t), 0)),
            pl.BlockSpec((LANES, GLA_QK), lambda b, t: (0, 0)),
            pl.BlockSpec((1, GLA_QK), lambda b, t: (0, 0)),
            pl.BlockSpec((1, GLA_DV), lambda b, t: (0, 0)),
            pl.BlockSpec((None, GLA_HEADS, GLA_DK, GLA_DV), lambda b, t: (b, 0, 0, 0)),
            pl.BlockSpec((GLA_QK, GLA_WIDTH), lambda b, t: (0, 0)),
        ],
        out_specs=[
            pl.BlockSpec((tb, GLA_WIDTH), lambda b, t: (row(b, t), 0)),
            pl.BlockSpec((None, GLA_HEADS, GLA_DK, GLA_DV), lambda b, t: (b, 0, 0, 0)),
        ],
        out_shape=[jax.ShapeDtypeStruct((n, GLA_WIDTH), act_dtype),
                   jax.ShapeDtypeStruct((seqs, GLA_HEADS, GLA_DK, GLA_DV), F32)],
        scratch_shapes=[pltpu.VMEM((GLA_HEADS, GLA_DV, GLA_DK), F32),
                        pltpu.VMEM((GLA_PAD + tb, GLA_QK), F32), pltpu.VMEM((GLA_PAD + tb, GLA_QK), F32),
                        pltpu.VMEM((GLA_PAD + tb, GLA_WIDTH), F32), pltpu.VMEM((tb, GLA_WIDTH), F32)],
        compiler_params=pltpu.CompilerParams(dimension_semantics=("arbitrary", "arbitrary")),
        name="gla",
    )(proj, proj, proj, proj, ga, wa2, ba, gn, s0, _gla_head_matrix())


HALO = 32


def _conv_kernel(cu_ref, cg_ref, cw_ref, cb_ref, lg_ref, lb_ref, buf_ref, o_ref, nb_ref, ext, *, tb):
    t = pl.program_id(1)
    hist = CONV_WIDTH - 1

    @pl.when(t == 0)
    def _():
        ext[HALO - hist:HALO, :] = buf_ref[...]

    @pl.when(t > 0)
    def _():
        ext[0:HALO, :] = ext[tb:tb + HALO, :]

    ext[HALO:HALO + tb, :] = cu_ref[...] * _sigmoid(cg_ref[...])
    y = jnp.broadcast_to(cb_ref[...], (tb, CONV_CH))
    for w in range(CONV_WIDTH):
        y = y + ext[HALO - hist + w:HALO - hist + w + tb, :] * cw_ref[w:w + 1, :]
    nb_ref[...] = ext[HALO + tb - hist:HALO + tb, :]
    mu = jnp.mean(y, axis=-1, keepdims=True)
    yc = y - mu
    var = jnp.mean(yc * yc, axis=-1, keepdims=True)
    yn = yc * lax.rsqrt(var + LN_EPS) * lg_ref[...] + lb_ref[...]
    o_ref[...] = _silu(yn).astype(o_ref.dtype)


def _conv(proj, cw, cb, lg, lb, buf0, *, seqs, seq_len, tb, act_dtype):
    n = proj.shape[0]
    steps = seq_len // tb
    row = lambda b, t: b * steps + t
    hist = CONV_WIDTH - 1
    vec = pl.BlockSpec((1, CONV_CH), lambda b, t: (0, 0))
    return pl.pallas_call(
        functools.partial(_conv_kernel, tb=tb),
        grid=(seqs, steps),
        in_specs=[
            pl.BlockSpec((tb, CONV_CH), lambda b, t: (row(b, t), OFF_CU // CONV_CH)),
            pl.BlockSpec((tb, CONV_CH), lambda b, t: (row(b, t), OFF_CG // CONV_CH)),
            pl.BlockSpec((CONV_WIDTH, CONV_CH), lambda b, t: (0, 0)),
            vec, vec, vec,
            pl.BlockSpec((None, hist, CONV_CH), lambda b, t: (b, 0, 0)),
        ],
        out_specs=[
            pl.BlockSpec((tb, CONV_CH), lambda b, t: (row(b, t), 0)),
            pl.BlockSpec((None, hist, CONV_CH), lambda b, t: (b, 0, 0)),
        ],
        out_shape=[jax.ShapeDtypeStruct((n, CONV_CH), act_dtype), jax.ShapeDtypeStruct((seqs, hist, CONV_CH), F32)],
        scratch_shapes=[pltpu.VMEM((HALO + tb, CONV_CH), F32)],
        compiler_params=pltpu.CompilerParams(dimension_semantics=("arbitrary", "arbitrary")),
        name="conv",
    )(proj, proj, cw, cb, lg, lb, buf0)


def _top_blocks_t(s, n_past, blk_id):
    nblk = float(s.shape[0])
    past = blk_id < n_past
    s = jnp.where(past, s, -jnp.inf)
    sel = jnp.zeros(s.shape, F32)
    for _ in range(MOBA_TOPK):
        m = jnp.max(s, axis=0, keepdims=True)
        first = jnp.min(jnp.where(s == m, blk_id, nblk), axis=0, keepdims=True)
        pick = (blk_id == first) & past
        sel = jnp.where(pick, 1.0, sel)
        s = jnp.where(pick, -jnp.inf, s)
    return sel


MOBA_UNROLL = 4
MOBA_HEADS_PER_STEP = 2


KM_CHAINS = 8


def _past_block_means(page_refs, km_ref, step, n_steps, blocks_per_step):
    total_blocks = km_ref.shape[0] * km_ref.shape[1]
    for n in range(blocks_per_step):
        g = step * blocks_per_step + n

        def write(n=n, g=g):
            s = jnp.zeros((MOBA_HEADS, HEAD_DIM), F32)
            for p in range(PAGES_PER_BLOCK):
                page = page_refs[n * PAGES_PER_BLOCK + p][...]
                part = jnp.sum(page.reshape(KM_CHAINS, PAGE_SIZE // KM_CHAINS, MOBA_HEADS, HEAD_DIM), axis=1)
                s = s + jnp.sum(part, axis=0)
            per_seq = jnp.int32(km_ref.shape[1])
            km_ref[lax.div(g, per_seq), lax.rem(g, per_seq)] = s / MOBA_BLOCK

        if total_blocks == n_steps * blocks_per_step:
            write()
        else:
            pl.when(g < total_blocks)(write)


def _moba_prompt_kernel(pt_ref, q_ref, k_ref, v_ref, *rest, nblk, n_steps, blocks_per_step):
    del pt_ref
    n_pages = blocks_per_step * PAGES_PER_BLOCK
    page_refs = rest[:n_pages]
    o_ref, km_ref, kmean_scr, kb_scr, vt_scr, l_scr = rest[n_pages:]
    step = (pl.program_id(0) * pl.num_programs(1) + pl.program_id(1)) * pl.num_programs(2) + pl.program_id(2)
    _past_block_means(page_refs, km_ref, step, n_steps, blocks_per_step)
    _moba_prompt_body(q_ref, k_ref, v_ref, o_ref, kmean_scr, kb_scr, vt_scr, l_scr, nblk=nblk)


def _moba_prompt_body(q_ref, k_ref, v_ref, o_ref, kmean_scr, kb_scr, vt_scr, l_scr, *, nblk):
    qi = pl.program_id(2)
    blk = MOBA_BLOCK
    heads = range(MOBA_HEADS_PER_STEP)
    hsl = [slice(h * HEAD_DIM, (h + 1) * HEAD_DIM) for h in heads]

    @pl.when(qi == 0)
    def _():
        for h in heads:
            for n in range(nblk):
                kblk = k_ref[n * blk:(n + 1) * blk, hsl[h]]
                kmean_scr[h, n:n + 1, :] = jnp.mean(kblk, axis=0, keepdims=True)
                kb_scr[h, n] = kblk.astype(BF16)
                vt_scr[h, n] = v_ref[n * blk:(n + 1) * blk, hsl[h]].T.astype(BF16)

    blk_id = lax.broadcasted_iota(jnp.int32, (nblk, blk), 0)
    blk_f = blk_id.astype(F32)
    scale = HEAD_DIM ** -0.5
    key = lax.broadcasted_iota(jnp.int32, (blk, blk), 0)
    qry = lax.broadcasted_iota(jnp.int32, (blk, blk), 1)

    def fold(x, op):
        return op(x.reshape(blk // SUBLANES, SUBLANES, blk), axis=0)

    qtb, sel, l_own = [], [], []
    for h in heads:
        qt = q_ref[:, hsl[h]].T
        scores = jnp.dot(kmean_scr[h], qt, preferred_element_type=F32, precision=HIGHEST)
        sel.append(_top_blocks_t(scores, qi.astype(F32), blk_f))
        qtb.append(qt.astype(BF16))
        l_own.append(jnp.where(key <= qry, jnp.dot(kb_scr[h, qi], qtb[h], preferred_element_type=F32) * scale, MASK_VALUE))

    def pass1(kj, mmax):
        out = []
        for h in heads:
            logits = jnp.dot(kb_scr[h, kj], qtb[h], preferred_element_type=F32) * scale
            picked = jnp.max(jnp.where(blk_id == kj, sel[h], 0.0), axis=0, keepdims=True)
            logits = jnp.where(picked > 0.5, logits, MASK_VALUE)
            l_scr[h, kj] = logits
            out.append(jnp.maximum(mmax[h], fold(logits, jnp.max)))
        return tuple(out)

    def pass2(kj, carry):
        out = []
        for h in heads:
            lsum, acc = carry[h]
            p = jnp.exp(l_scr[h, kj] - m[h])
            out.append((lsum + fold(p, jnp.sum), acc + jnp.dot(vt_scr[h, kj], p.astype(BF16), preferred_element_type=F32)))
        return tuple(out)

    def grouped(body, carry):
        def group(g, c):
            for u in range(MOBA_UNROLL):
                c = body(g * MOBA_UNROLL + u, c)
            return c
        return lax.fori_loop(0, (qi + MOBA_UNROLL - 1) // MOBA_UNROLL, group, carry)

    mmax = grouped(pass1, tuple(fold(l_own[h], jnp.max) for h in heads))
    m = [jnp.max(mmax[h], axis=0, keepdims=True) for h in heads]
    p_own = [jnp.exp(l_own[h] - m[h]) for h in heads]
    res = grouped(pass2, tuple((fold(p_own[h], jnp.sum), jnp.dot(vt_scr[h, qi], p_own[h].astype(BF16), preferred_element_type=F32))
                               for h in heads))
    for h in heads:
        lsum, acc = res[h]
        o_ref[:, hsl[h]] = (acc / jnp.sum(lsum, axis=0, keepdims=True)).T.astype(o_ref.dtype)


def _moba_prompt(proj, page_table, cache_k, *, layer, seqs, seq_len):
    n = proj.shape[0]
    nblk = seq_len // MOBA_BLOCK
    assert nblk % MOBA_UNROLL == 0
    bsz, n_pages = page_table.shape
    blocks_per_seq = n_pages // PAGES_PER_BLOCK
    total_blocks = bsz * blocks_per_seq
    hps = MOBA_HEADS_PER_STEP
    wide = hps * HEAD_DIM
    steps = seqs * (MOBA_HEADS // hps) * nblk
    blocks_per_step = -(-total_blocks // steps)
    pages_per_step = blocks_per_step * PAGES_PER_BLOCK
    hb = lambda off: off // wide
    vm = _vmem_limit(4 * _nbytes((seq_len, wide), F32), 2 * _nbytes((seq_len, wide), BF16),
                     hps * (nblk + 12) * _nbytes((MOBA_BLOCK, MOBA_BLOCK), F32),
                     2 * pages_per_step * _nbytes((PAGE_SIZE, MOBA_WIDTH), F32), 2 * total_blocks * _nbytes((MOBA_WIDTH,), F32))

    def page_spec(p):
        def index(b, h, i, pt):
            page = ((b * (MOBA_HEADS // hps) + h) * nblk + i) * pages_per_step + p
            return (layer, pt[jnp.minimum(page, bsz * n_pages - 1)], 0, 0, 0)
        return pl.BlockSpec((None, None, PAGE_SIZE, MOBA_HEADS, HEAD_DIM), index)

    grid_spec = pltpu.PrefetchScalarGridSpec(
        num_scalar_prefetch=1,
        grid=(seqs, MOBA_HEADS // hps, nblk),
        in_specs=[
            pl.BlockSpec((MOBA_BLOCK, wide), lambda b, h, i, pt: (b * nblk + i, hb(OFF_MQ) + h)),
            pl.BlockSpec((seq_len, wide), lambda b, h, i, pt: (b, hb(OFF_MK) + h)),
            pl.BlockSpec((seq_len, wide), lambda b, h, i, pt: (b, hb(OFF_MV) + h)),
        ] + [page_spec(p) for p in range(pages_per_step)],
        out_specs=[
            pl.BlockSpec((MOBA_BLOCK, wide), lambda b, h, i, pt: (b * nblk + i, h)),
            pl.BlockSpec((bsz, blocks_per_seq, MOBA_HEADS, HEAD_DIM), lambda b, h, i, pt: (0, 0, 0, 0)),
        ],
        scratch_shapes=[pltpu.VMEM((hps, nblk, HEAD_DIM), F32), pltpu.VMEM((hps, nblk, MOBA_BLOCK, HEAD_DIM), BF16),
                        pltpu.VMEM((hps, nblk, HEAD_DIM, MOBA_BLOCK), BF16), pltpu.VMEM((hps, nblk, MOBA_BLOCK, MOBA_BLOCK), F32)],
    )
    return pl.pallas_call(
        functools.partial(_moba_prompt_kernel, nblk=nblk, n_steps=steps, blocks_per_step=blocks_per_step),
        grid_spec=grid_spec,
        out_shape=[jax.ShapeDtypeStruct((n, MOBA_WIDTH), BF16),
                   jax.ShapeDtypeStruct((bsz, blocks_per_seq, MOBA_HEADS, HEAD_DIM), F32)],
        compiler_params=pltpu.CompilerParams(dimension_semantics=("arbitrary", "arbitrary", "arbitrary"), vmem_limit_bytes=vm),
        name="moba_prompt",
    )(page_table.reshape(-1), proj, proj, proj, *([cache_k] * pages_per_step))


def _select_kernel(qa_ref, qb_ref, km_ref, o_ref, *, nblk):
    rows = qa_ref.shape[0]
    half = MOBA_HEADS // 2
    lane = lax.broadcasted_iota(jnp.int32, (rows, nblk), 1).astype(F32)
    out_lane = lax.broadcasted_iota(jnp.int32, (rows, LANES), 1)
    for h in range(MOBA_HEADS):
        q_ref = qa_ref if h < half else qb_ref
        q = q_ref[:, (h % half) * HEAD_DIM:(h % half + 1) * HEAD_DIM]
        s = _dot_nt(q, km_ref[:, h, :], precision=HIGHEST)
        out = jnp.zeros((rows, LANES), F32)
        for j in range(MOBA_TOPK):
            m = jnp.max(s, axis=1, keepdims=True)
            first = jnp.min(jnp.where(s == m, lane, float(nblk)), axis=1, keepdims=True)
            out = jnp.where(out_lane == j, first, out)
            s = jnp.where(lane == first, -jnp.inf, s)
        o_ref[h] = out.astype(jnp.int32)


def _sample_select(proj, kmean_l, *, seqs, seq_len):
    nblk = kmean_l.shape[1]
    half_w = MOBA_WIDTH // 2
    return pl.pallas_call(
        functools.partial(_select_kernel, nblk=nblk),
        grid=(seqs,),
        in_specs=[
            pl.BlockSpec((seq_len, half_w), lambda b: (b, OFF_MQ // half_w)),
            pl.BlockSpec((seq_len, half_w), lambda b: (b, OFF_MQ // half_w + 1)),
            pl.BlockSpec((None, nblk, MOBA_HEADS, HEAD_DIM), lambda b: (b, 0, 0, 0)),
        ],
        out_specs=pl.BlockSpec((None, MOBA_HEADS, seq_len, LANES), lambda b: (b, 0, 0, 0)),
        out_shape=jax.ShapeDtypeStruct((seqs, MOBA_HEADS, seq_len, LANES), jnp.int32),
        compiler_params=pltpu.CompilerParams(dimension_semantics=("arbitrary",)),
        name="sample_select",
    )(proj, proj, kmean_l)


def _sample_attn_kernel(idx_ref, pt_ref, q_ref, kn_ref, vn_ref, ck_hbm, cv_hbm, o_ref, kbuf, vbuf, sem, *, layer, seq_len, n_pages):
    step = pl.program_id(0) * MOBA_HEADS + pl.program_id(1)
    n_steps = pl.num_programs(0) * MOBA_HEADS
    nsel = seq_len * MOBA_TOPK
    npg = nsel * PAGES_PER_BLOCK
    slot = step % 2

    def copies(s, slot_s, i):
        b, h = s // MOBA_HEADS, s % MOBA_HEADS
        blk = idx_ref[s * nsel + i // PAGES_PER_BLOCK]
        pg = pt_ref[b * n_pages + blk * PAGES_PER_BLOCK + i % PAGES_PER_BLOCK]
        return (pltpu.make_async_copy(ck_hbm.at[layer, pg, :, h, :], kbuf.at[slot_s, i], sem.at[0, slot_s]),
                pltpu.make_async_copy(cv_hbm.at[layer, pg, :, h, :], vbuf.at[slot_s, i], sem.at[1, slot_s]))

    def fetch(s, slot_s):
        for i in range(npg):
            ck, cv = copies(s, slot_s, i)
            ck.start()
            cv.start()

    @pl.when(step == 0)
    def _():
        fetch(step, slot)

    @pl.when(step + 1 < n_steps)
    def _():
        fetch(step + 1, 1 - slot)

    for i in range(npg):
        ck, cv = copies(step, slot, i)
        ck.wait()
        cv.wait()

    q = q_ref[...]
    scale = HEAD_DIM ** -0.5
    nkeys = npg * PAGE_SIZE
    per_q = MOBA_TOPK * MOBA_BLOCK
    kall = kbuf[slot].reshape(nkeys, HEAD_DIM)
    vall = vbuf[slot].reshape(nkeys, HEAD_DIM)
    logits = _dot_nt(q, kall) * scale
    row = lax.broadcasted_iota(jnp.int32, (seq_len, nkeys), 0)
    col = lax.broadcasted_iota(jnp.int32, (seq_len, nkeys), 1)
    allowed = (col >= row * per_q) & (col < (row + 1) * per_q)
    own = _dot_nt(q, kn_ref[...]) * scale
    orow = lax.broadcasted_iota(jnp.int32, (seq_len, seq_len), 0)
    ocol = lax.broadcasted_iota(jnp.int32, (seq_len, seq_len), 1)
    oallowed = ocol <= orow
    m = jnp.maximum(jnp.max(jnp.where(allowed, logits, MASK_VALUE), axis=1, keepdims=True),
                    jnp.max(jnp.where(oallowed, own, MASK_VALUE), axis=1, keepdims=True))
    p = jnp.where(allowed, jnp.exp(logits - m), 0.0)
    po = jnp.where(oallowed, jnp.exp(own - m), 0.0)
    l = jnp.sum(p, axis=1, keepdims=True) + jnp.sum(po, axis=1, keepdims=True)
    acc = jnp.dot(p, vall, preferred_element_type=F32) + jnp.dot(po, vn_ref[...], preferred_element_type=F32)
    o_ref[...] = (acc / l).astype(o_ref.dtype)


def _sample_attn(idx, page_table, proj, cache_k, cache_v, *, layer, seqs, seq_len):
    n = proj.shape[0]
    n_pages = page_table.shape[1]
    npg = seq_len * MOBA_TOPK * PAGES_PER_BLOCK
    hb = lambda off: off // HEAD_DIM
    grid_spec = pltpu.PrefetchScalarGridSpec(
        num_scalar_prefetch=2,
        grid=(seqs, MOBA_HEADS),
        in_specs=[
            pl.BlockSpec((seq_len, HEAD_DIM), lambda b, h, ix, pt: (b, hb(OFF_MQ) + h)),
            pl.BlockSpec((seq_len, HEAD_DIM), lambda b, h, ix, pt: (b, hb(OFF_MK) + h)),
            pl.BlockSpec((seq_len, HEAD_DIM), lambda b, h, ix, pt: (b, hb(OFF_MV) + h)),
            pl.BlockSpec(memory_space=pl.ANY),
            pl.BlockSpec(memory_space=pl.ANY),
        ],
        out_specs=pl.BlockSpec((seq_len, HEAD_DIM), lambda b, h, ix, pt: (b, h)),
        scratch_shapes=[pltpu.VMEM((2, npg, PAGE_SIZE, HEAD_DIM), F32), pltpu.VMEM((2, npg, PAGE_SIZE, HEAD_DIM), F32),
                        pltpu.SemaphoreType.DMA((2, 2))],
    )
    vm = _vmem_limit(10 * npg * _nbytes((PAGE_SIZE, HEAD_DIM), F32))
    return pl.pallas_call(
        functools.partial(_sample_attn_kernel, layer=layer, seq_len=seq_len, n_pages=n_pages),
        grid_spec=grid_spec,
        out_shape=jax.ShapeDtypeStruct((n, MOBA_WIDTH), F32),
        compiler_params=pltpu.CompilerParams(dimension_semantics=("arbitrary", "arbitrary"), vmem_limit_bytes=vm),
        name="sample_attn",
    )(idx, page_table.reshape(-1), proj, proj, proj, cache_k, cache_v)


def _outproj_kernel(og_ref, om_ref, oc_ref, w_ref, x_ref, gt_ref, g_ref, o_ref):
    acc = jnp.dot(og_ref[...].astype(BF16), w_ref[0:GLA_WIDTH, :], preferred_element_type=F32)
    acc = acc + jnp.dot(om_ref[...].astype(BF16), w_ref[GLA_WIDTH:GLA_WIDTH + MOBA_WIDTH, :], preferred_element_type=F32)
    acc = acc + jnp.dot(oc_ref[...].astype(BF16), w_ref[GLA_WIDTH + MOBA_WIDTH:D_MODEL, :], preferred_element_type=F32)
    o_ref[...] = x_ref[...] + gt_ref[...] * _rms(acc, g_ref[...])


def _outproj(og, om, oc, w, x, gt, g, *, layer, tm, mod_spec):
    n = x.shape[0]
    mod1 = pl.BlockSpec(mod_spec.block_shape, lambda i: mod_spec.index_map(i, 0))
    vm = _vmem_limit(2 * _nbytes((D_MODEL, D_MODEL), BF16), 2 * _nbytes((tm, D_MODEL), BF16), 6 * _nbytes((tm, D_MODEL), F32))
    return pl.pallas_call(
        _outproj_kernel,
        grid=(n // tm,),
        in_specs=[
            pl.BlockSpec((tm, GLA_WIDTH), lambda i: (i, 0)),
            pl.BlockSpec((tm, MOBA_WIDTH), lambda i: (i, 0)),
            pl.BlockSpec((tm, CONV_CH), lambda i: (i, 0)),
            pl.BlockSpec((None, D_MODEL, D_MODEL), lambda i: (layer, 0, 0)),
            pl.BlockSpec((tm, D_MODEL), lambda i: (i, 0)),
            mod1,
            pl.BlockSpec((1, D_MODEL), lambda i: (0, 0)),
        ],
        out_specs=pl.BlockSpec((tm, D_MODEL), lambda i: (i, 0)),
        out_shape=jax.ShapeDtypeStruct((n, D_MODEL), F32),
        compiler_params=pltpu.CompilerParams(dimension_semantics=("arbitrary",), vmem_limit_bytes=vm),
        name="outproj",
    )(og, om, oc, w, x, gt, g)


FFN_TF = 512


def _ffn_kernel(x_ref, sh_ref, sc_ref, gt_ref, gpre_ref, gpost_ref, w1_ref, w3_ref, w2_ref, o_ref, h_scr, acc_scr):
    j = pl.program_id(1)

    @pl.when(j == 0)
    def _():
        h = _rms(x_ref[...], gpre_ref[...]) * (1.0 + sc_ref[...]) + sh_ref[...]
        h_scr[...] = h.astype(BF16)
        acc_scr[...] = jnp.zeros_like(acc_scr)

    h = h_scr[...]
    u = jnp.dot(h, w1_ref[...], preferred_element_type=F32)
    g = jnp.dot(h, w3_ref[...], preferred_element_type=F32)
    a = (_silu(u) * g).astype(BF16)
    acc_scr[...] += jnp.dot(a, w2_ref[...], preferred_element_type=F32)

    @pl.when(j == pl.num_programs(1) - 1)
    def _():
        o_ref[...] = x_ref[...] + gt_ref[...] * _rms(acc_scr[...], gpost_ref[...])


def _ffn(x, sh, sc, gt, gpre, gpost, w1, w3, w2, *, layer, tm, mod_spec):
    n = x.shape[0]
    vec = pl.BlockSpec((1, D_MODEL), lambda i, j: (0, 0))
    vm = _vmem_limit(4 * _nbytes((tm, D_MODEL), F32), _nbytes((tm, D_MODEL), BF16), _nbytes((tm, D_MODEL), F32),
                     6 * _nbytes((D_MODEL, FFN_TF), BF16), 3 * _nbytes((tm, FFN_TF), F32))
    return pl.pallas_call(
        _ffn_kernel,
        grid=(n // tm, D_FF // FFN_TF),
        in_specs=[
            pl.BlockSpec((tm, D_MODEL), lambda i, j: (i, 0)),
            mod_spec, mod_spec, mod_spec, vec, vec,
            pl.BlockSpec((None, D_MODEL, FFN_TF), lambda i, j: (layer, 0, j)),
            pl.BlockSpec((None, D_MODEL, FFN_TF), lambda i, j: (layer, 0, j)),
            pl.BlockSpec((None, FFN_TF, D_MODEL), lambda i, j: (layer, j, 0)),
        ],
        out_specs=pl.BlockSpec((tm, D_MODEL), lambda i, j: (i, 0)),
        out_shape=jax.ShapeDtypeStruct((n, D_MODEL), F32),
        scratch_shapes=[pltpu.VMEM((tm, D_MODEL), BF16), pltpu.VMEM((tm, D_MODEL), F32)],
        compiler_params=pltpu.CompilerParams(dimension_semantics=("arbitrary", "arbitrary"), vmem_limit_bytes=vm),
        name="ffn",
    )(x, sh, sc, gt, gpre, gpost, w1, w3, w2)


def _rope_tables(pos):
    inv = ROPE_THETA ** (-jnp.arange(0, HEAD_DIM, 2, dtype=F32) / HEAD_DIM)
    ang = pos.astype(F32)[:, None] * inv[None, :]
    cos, sin = jnp.cos(ang), jnp.sin(ang)
    return jnp.concatenate([cos, cos], axis=-1), jnp.concatenate([-sin, sin], axis=-1)


PROMPT_TM = 512
PROMPT_TM_IN = 1024
PROMPT_TB = 256


def kernel(x_prompt, x_sample, c_prompt, c_sample, cache_k, cache_v, state_gla, state_conv, page_table, w_ada, b_ada, g_pre_mix, g_post_mix, g_pre_ffn, g_post_ffn, w_in, w_a2, b_a, gla_norm_g, conv_w, conv_b, conv_ln_g, conv_ln_b, w_out, w_ffn1, w_ffn3, w_ffn2):
    bp, tp, _ = x_prompt.shape
    bs, ts, _ = x_sample.shape
    past = page_table.shape[1] * cache_k.shape[2]
    np_, ns = bp * tp, bs * ts
    assert tp % PROMPT_TM == 0 and tp % PROMPT_TM_IN == 0 and tp % MOBA_BLOCK == 0 and past % MOBA_BLOCK == 0 and ts <= CONV_WIDTH - 1

    w_main = jnp.concatenate([w_in[:, :, :GATE_COL0], w_in[:, :, GATE_COL0 + GLA_GATE_RANK:]], axis=-1).astype(BF16)
    w_ga = jnp.pad(w_in[:, :, GATE_COL0:GATE_COL0 + GLA_GATE_RANK], ((0, 0), (0, 0), (0, LANES - GLA_GATE_RANK))).astype(BF16)
    wa2 = jnp.pad(w_a2, ((0, 0), (0, LANES - GLA_GATE_RANK), (0, 0)))
    w_out_b, w1_b, w3_b, w2_b = (w.astype(BF16) for w in (w_out, w_ffn1, w_ffn3, w_ffn2))

    rows = -(-(bp + bs) // SUBLANES) * SUBLANES
    c_all = jnp.pad(jnp.concatenate([c_prompt, c_sample], axis=0), ((0, rows - bp - bs), (0, 0)))
    mod = _modulation(c_all, w_ada, b_ada).reshape(DEPTH, rows, 6, D_MODEL)
    mod_p = mod[:, :bp].reshape(DEPTH, bp, 6, 1, D_MODEL)
    mod_s = jnp.repeat(mod[:, bp:bp + bs], ts, axis=1).transpose(0, 2, 1, 3)

    def mod_spec_p(tm):
        return pl.BlockSpec((None, 1, D_MODEL), lambda i, j: (i // (tp // tm), 0, 0))

    mod_spec_s = pl.BlockSpec((ns, D_MODEL), lambda i, j: (0, 0))
    cos_p, sin_p = _rope_tables(jnp.arange(tp))
    cos_s, sin_s = _rope_tables(past + jnp.arange(ns) % ts)
    rope_spec_p = pl.BlockSpec((PROMPT_TM_IN, HEAD_DIM), lambda i, j: (i % (tp // PROMPT_TM_IN), 0))
    rope_spec_s = pl.BlockSpec((ns, HEAD_DIM), lambda i, j: (0, 0))

    gla0 = jnp.zeros((bp, GLA_HEADS, GLA_DK, GLA_DV), F32)
    buf0 = jnp.zeros((bp, CONV_WIDTH - 1, CONV_CH), F32)

    def vec(a):
        return a.reshape(1, -1)

    xp = x_prompt.reshape(np_, D_MODEL)
    xs = x_sample.reshape(ns, D_MODEL)
    outs = {k: [] for k in ("sp", "cp", "ss", "cs")}
    kv_p = [jnp.zeros((DEPTH, np_, MOBA_HEADS, HEAD_DIM), F32) for _ in range(2)]
    kv_s = [jnp.zeros((DEPTH, ns, MOBA_HEADS, HEAD_DIM), F32) for _ in range(2)]
    for l in range(DEPTH):
        mp = [mod_p[l, :, k] for k in range(6)]
        proj, ga, *kv_p = _inproj(xp, mp[0], mp[1], vec(g_pre_mix[l]), w_main, w_ga, cos_p, sin_p, kv_p,
                                  layer=l, tm=PROMPT_TM_IN, mod_spec=mod_spec_p(PROMPT_TM_IN), rope_spec=rope_spec_p)
        og, st = _gla(proj, ga, wa2[l], vec(b_a[l]), vec(gla_norm_g[l]), gla0, seqs=bp, seq_len=tp, tb=PROMPT_TB, act_dtype=BF16)
        oc, nb = _conv(proj, conv_w[l], vec(conv_b[l]), vec(conv_ln_g[l]), vec(conv_ln_b[l]), buf0, seqs=bp, seq_len=tp, tb=PROMPT_TB, act_dtype=BF16)
        om, kmean_past = _moba_prompt(proj, page_table, cache_k, layer=l, seqs=bp, seq_len=tp)
        xp = _outproj(og, om, oc, w_out_b, xp, mp[2], vec(g_post_mix[l]), layer=l, tm=PROMPT_TM, mod_spec=mod_spec_p(PROMPT_TM))
        xp = _ffn(xp, mp[3], mp[4], mp[5], vec(g_pre_ffn[l]), vec(g_post_ffn[l]), w1_b, w3_b, w2_b,
                  layer=l, tm=PROMPT_TM, mod_spec=mod_spec_p(PROMPT_TM))
        outs["sp"].append(st)
        outs["cp"].append(nb)

        ms = [mod_s[l, k] for k in range(6)]
        proj, ga, *kv_s = _inproj(xs, ms[0], ms[1], vec(g_pre_mix[l]), w_main, w_ga, cos_s, sin_s, kv_s,
                                  layer=l, tm=ns, mod_spec=mod_spec_s, rope_spec=rope_spec_s)
        og, st = _gla(proj, ga, wa2[l], vec(b_a[l]), vec(gla_norm_g[l]), state_gla[l], seqs=bs, seq_len=ts, tb=ts, act_dtype=F32)
        oc, nb = _conv(proj, conv_w[l], vec(conv_b[l]), vec(conv_ln_g[l]), vec(conv_ln_b[l]), state_conv[l], seqs=bs, seq_len=ts, tb=ts, act_dtype=F32)
        sel = _sample_select(proj, kmean_past, seqs=bs, seq_len=ts)
        idx = sel[:, :, :, :MOBA_TOPK].reshape(-1)
        om = _sample_attn(idx, page_table, proj, cache_k, cache_v, layer=l, seqs=bs, seq_len=ts)
        xs = _outproj(og, om, oc, w_out_b, xs, ms[2], vec(g_post_mix[l]), layer=l, tm=ns, mod_spec=mod_spec_s)
        xs = _ffn(xs, ms[3], ms[4], ms[5], vec(g_pre_ffn[l]), vec(g_post_ffn[l]), w1_b, w3_b, w2_b,
                  layer=l, tm=ns, mod_spec=mod_spec_s)
        outs["ss"].append(st)
        outs["cs"].append(nb)

    shape_p = (DEPTH, bp, tp, MOBA_HEADS, HEAD_DIM)
    shape_s = (DEPTH, bs, ts, MOBA_HEADS, HEAD_DIM)
    return (xp.reshape(bp, tp, D_MODEL), xs.reshape(bs, ts, D_MODEL),
            kv_p[0].reshape(shape_p), kv_p[1].reshape(shape_p),
            jnp.stack(outs["sp"]), jnp.stack(outs["cp"]),
            kv_s[0].reshape(shape_s), kv_s[1].reshape(shape_s),
            jnp.stack(outs["ss"]), jnp.stack(outs["cs"]))
```
